```python
import math
import jax, jax.numpy as jnp
from jax import lax
import numpy as np

D_MODEL = 2048
BATCH = 8
SEQ = 2048
DEPTH = 2

N_MEM = 256
MAX_POS_OFFSET = 4096
ROPE_THETA = 10000.0
NORM_EPS = 1e-6
BLOCK_Q = 128
NEG_INF = -1e30
FORCE_SCORE = 1e6

A_HEADS = 8
A_KV_HEADS = 2
A_HEAD_DIM = 64
A_WINDOW = 128

B_HEADS = 8
B_KV_GROUPS = 2
B_HEAD_DIM = 128
CMP_LEN = 32
CMP_STRIDE = 16
SLC_LEN = 64
SLC_TOPN = 8
SLC_QBLOCK = 64
B_WINDOW = 512

C_HEADS = 4
C_HEAD_DIM = 64

X_HEADS = 4
X_HEAD_DIM = 128
X_WIDTH = X_HEADS * X_HEAD_DIM

N_GROUPS = 4
EXPERTS_PER_GROUP = 4
N_EXPERTS = N_GROUPS * EXPERTS_PER_GROUP
TOP_K_IN_GROUP = 2
D_FF_EXPERT = 1024

A_Q_W = A_HEADS * A_HEAD_DIM
A_KV_W = A_KV_HEADS * A_HEAD_DIM
B_Q_W = B_HEADS * B_HEAD_DIM
B_KV_W = B_KV_GROUPS * B_HEAD_DIM
B_GATE_W = 3 * B_HEADS
C_QK_W = C_HEADS * 2 * C_HEAD_DIM
C_V_W = C_HEADS * 2 * C_HEAD_DIM
MERGE_W = 3 * D_MODEL
IN_SIZES = (A_Q_W, A_KV_W, A_KV_W,
            B_Q_W, B_KV_W, B_KV_W, B_KV_W, B_KV_W, B_KV_W, B_KV_W, B_GATE_W,
            C_QK_W, C_QK_W, C_V_W,
            MERGE_W)
IN_WIDTH = sum(IN_SIZES)

kernel_name = 'hybrid_swa_nsa_diff_memx_hmoe'


def _split_points():
    pts, acc = [], 0
    for s in IN_SIZES[:-1]:
        acc += s
        pts.append(acc)
    return pts


def rms_norm(x, g):
    xf = x.astype(jnp.float32)
    y = xf * lax.rsqrt(jnp.mean(xf * xf, axis=-1, keepdims=True) + NORM_EPS)
    return (y * g.astype(jnp.float32)).astype(x.dtype)


def split_heads(t, h):
    b, s, _ = t.shape
    return t.reshape(b, s, h, -1).transpose(0, 2, 1, 3)


def merge_heads(o):
    b, h, s, d = o.shape
    return o.transpose(0, 2, 1, 3).reshape(b, s, h * d)


def rope(x, pos):
    d = x.shape[-1]
    inv = jnp.power(ROPE_THETA, -jnp.arange(0, d, 2, dtype=jnp.float32) / d)
    ang = pos.astype(jnp.float32)[:, None, :, None] * inv
    cos, sin = jnp.cos(ang), jnp.sin(ang)
    xf = x.astype(jnp.float32)
    x1, x2 = xf[..., : d // 2], xf[..., d // 2:]
    return jnp.concatenate([x1 * cos - x2 * sin, x2 * cos + x1 * sin], axis=-1).astype(x.dtype)


def banded_gqa_attention(q, k, v, window, sink):
    b, h, s, d = q.shape
    g = k.shape[1]
    r = h // g
    nb = s // BLOCK_Q
    p_prev = -(-window // BLOCK_Q)
    pad = ((0, 0), (0, 0), (p_prev * BLOCK_Q, 0), (0, 0))
    kp = jnp.pad(k, pad).reshape(b, g, nb + p_prev, BLOCK_Q, d)
    vp = jnp.pad(v, pad).reshape(b, g, nb + p_prev, BLOCK_Q, v.shape[-1])
    kb = jnp.concatenate([kp[:, :, i:i + nb] for i in range(p_prev + 1)], axis=3)
    vb = jnp.concatenate([vp[:, :, i:i + nb] for i in range(p_prev + 1)], axis=3)
    qb = q.reshape(b, g, r, nb, BLOCK_Q, d)
    sc = jnp.einsum('bgrnqd,bgnkd->bgrnqk', qb, kb).astype(jnp.float32) * (d ** -0.5)
    qpos = jnp.arange(nb)[:, None] * BLOCK_Q + jnp.arange(BLOCK_Q)[None, :]
    kpos = (jnp.arange(nb)[:, None] - p_prev) * BLOCK_Q + jnp.arange((p_prev + 1) * BLOCK_Q)[None, :]
    rel = qpos[:, :, None] - kpos[:, None, :]
    ok = (rel >= 0) & (rel < window) & (kpos[:, None, :] >= 0)
    sc = jnp.where(ok, sc, NEG_INF)
    if sink is None:
        p = jax.nn.softmax(sc, axis=-1)
    else:
        sk = sink.astype(jnp.float32).reshape(1, g, r, 1, 1, 1)
        m = jnp.maximum(jnp.max(sc, axis=-1, keepdims=True), sk)
        e = jnp.exp(sc - m)
        p = e / (jnp.sum(e, axis=-1, keepdims=True) + jnp.exp(sk - m))
    o = jnp.einsum('bgrnqk,bgnkd->bgrnqd', p.astype(vb.dtype), vb)
    return o.reshape(b, h, s, v.shape[-1])


def swa_sink_mixer(q_in, k_in, v_in, sinks, pos):
    q = rope(split_heads(q_in, A_HEADS), pos)
    k = rope(split_heads(k_in, A_KV_HEADS), pos)
    v = split_heads(v_in, A_KV_HEADS)
    return merge_heads(banded_gqa_attention(q, k, v, A_WINDOW, sinks))


def compress_blocks(t, pos_emb, w1, w2):
    b, g, s, d = t.shape
    nc = (s - CMP_LEN) // CMP_STRIDE + 1
    idx = jnp.arange(nc)[:, None] * CMP_STRIDE + jnp.arange(CMP_LEN)[None, :]
    blocks = t[:, :, idx] + pos_emb
    hid = jax.nn.silu(blocks.reshape(b, g, nc, CMP_LEN * d) @ w1)
    return hid @ w2


def nsa_mixer(q_in, kc_in, vc_in, ks_in, vs_in, kw_in, vw_in, gate_in, pos,
              cmp_pos_k, cmp_pos_v, phi_k1, phi_k2, phi_v1, phi_v2):
    b, s, _ = q_in.shape
    g, r, d = B_KV_GROUPS, B_HEADS // B_KV_GROUPS, B_HEAD_DIM
    scale = d ** -0.5
    t = jnp.arange(s)
    q = rope(split_heads(q_in, B_HEADS), pos)
    qg = q.reshape(b, g, r, s, d)

    kc = compress_blocks(rope(split_heads(kc_in, g), pos), cmp_pos_k, phi_k1, phi_k2)
    vc = compress_blocks(split_heads(vc_in, g), cmp_pos_v, phi_v1, phi_v2)
    nc = kc.shape[2]
    c_start = jnp.arange(nc) * CMP_STRIDE
    cmp_ok = t[:, None] >= (c_start + CMP_LEN - 1)[None, :]
    s_cmp = jnp.einsum('bgrsd,bgcd->bgrsc', qg, kc).astype(jnp.float32) * scale
    p_cmp = jnp.where(cmp_ok, jax.nn.softmax(jnp.where(cmp_ok, s_cmp, NEG_INF), axis=-1), 0.0)
    o_cmp = jnp.einsum('bgrsc,bgcd->bgrsd', p_cmp.astype(vc.dtype), vc).reshape(b, B_HEADS, s, d)

    ns = s // SLC_LEN
    n_sel = min(SLC_TOPN, ns)
    j = jnp.arange(ns)
    overlap = ((c_start[:, None] < (j[None, :] + 1) * SLC_LEN)
               & (c_start[:, None] + CMP_LEN > j[None, :] * SLC_LEN)).astype(jnp.float32)
    imp = jnp.einsum('bgrsc,cj->bgsj', p_cmp, overlap)
    cur = (t // SLC_LEN)[:, None]
    forced = (j[None, :] == 0) | (j[None, :] == cur) | (j[None, :] == cur - 1)
    future = j[None, :] * SLC_LEN > t[:, None]
    imp = jnp.where(future, -1.0, jnp.where(forced, FORCE_SCORE, imp))
    _, sel = lax.top_k(imp, n_sel)

    ks = rope(split_heads(ks_in, g), pos).reshape(b, g, ns, SLC_LEN, d)
    vs = split_heads(vs_in, g).reshape(b, g, ns, SLC_LEN, d)
    nqb = s // SLC_QBLOCK
    q_blocks = jnp.moveaxis(qg.reshape(b, g, r, nqb, SLC_QBLOCK, d), 3, 0)
    sel_blocks = jnp.moveaxis(sel.reshape(b, g, nqb, SLC_QBLOCK, n_sel), 2, 0)
    t_blocks = t.reshape(nqb, SLC_QBLOCK)
    bi = jnp.arange(b)[:, None, None, None]
    gi = jnp.arange(g)[None, :, None, None]

    def attend_selected(args):
        qb, sb, tq = args
        kg = ks[bi, gi, sb]
        vg = vs[bi, gi, sb]
        kpos = sb[..., None] * SLC_LEN + jnp.arange(SLC_LEN)
        ok = kpos <= tq[None, None, :, None, None]
        sc = jnp.einsum('bgrqd,bgqnld->bgrqnl', qb, kg).astype(jnp.float32) * scale
        sc = jnp.where(ok[:, :, None], sc, NEG_INF)
        p = jax.nn.softmax(sc.reshape(b, g, r, SLC_QBLOCK, n_sel * SLC_LEN), axis=-1).reshape(sc.shape)
        return jnp.einsum('bgrqnl,bgqnld->bgrqd', p.astype(vg.dtype), vg)

    o_slc = lax.map(attend_selected, (q_blocks, sel_blocks, t_blocks))
    o_slc = jnp.moveaxis(o_slc, 0, 3).reshape(b, B_HEADS, s, d)

    kw = rope(split_heads(kw_in, g), pos)
    vw = split_heads(vw_in, g)
    o_win = banded_gqa_attention(q, kw, vw, B_WINDOW, None)

    gts = jax.nn.sigmoid(gate_in.astype(jnp.float32)).reshape(b, s, B_HEADS, 3).transpose(0, 2, 1, 3)
    gts = gts.astype(q.dtype)
    o = gts[..., 0:1] * o_cmp + gts[..., 1:2] * o_slc + gts[..., 2:3] * o_win
    return merge_heads(o)


def diff_mixer(q_in, k_in, v_in, pos, lq1, lk1, lq2, lk2, g_sub, lambda_init):
    b, s, _ = q_in.shape
    h, d = C_HEADS, C_HEAD_DIM
    scale = d ** -0.5
    q = rope(split_heads(q_in, 2 * h), pos).reshape(b, h, 2, s, d)
    k = rope(split_heads(k_in, 2 * h), pos).reshape(b, h, 2, s, d)
    v = split_heads(v_in, h)
    lam = (jnp.exp(jnp.sum(lq1.astype(jnp.float32) * lk1.astype(jnp.float32)))
           - jnp.exp(jnp.sum(lq2.astype(jnp.float32) * lk2.astype(jnp.float32))) + lambda_init)
    nb = s // BLOCK_Q
    q_blocks = jnp.moveaxis(q.reshape(b, h, 2, nb, BLOCK_Q, d), 3, 0)
    t_blocks = jnp.arange(s).reshape(nb, BLOCK_Q)
    kpos = jnp.arange(s)

    def attend(args):
        qb, tq = args
        sc = jnp.einsum('bhcqd,bhckd->bhcqk', qb, k).astype(jnp.float32) * scale
        sc = jnp.where(kpos[None, :] <= tq[:, None], sc, NEG_INF)
        a = jax.nn.softmax(sc, axis=-1)
        w = a[:, :, 0] - lam * a[:, :, 1]
        return jnp.einsum('bhqk,bhkd->bhqd', w.astype(v.dtype), v)

    o = lax.map(attend, (q_blocks, t_blocks))
    o = jnp.moveaxis(o, 0, 2).reshape(b, h, s, 2 * d)
    o = rms_norm(o, g_sub) * (1.0 - lambda_init)
    return merge_heads(o)


def memory_cross_attention(hx, hm, w_xq, w_xkv, w_xo):
    q = split_heads(hx @ w_xq, X_HEADS)
    kv = hm @ w_xkv
    k = split_heads(kv[..., :X_WIDTH], X_HEADS)
    v = split_heads(kv[..., X_WIDTH:], X_HEADS)
    sc = jnp.einsum('bhsd,bhmd->bhsm', q, k).astype(jnp.float32) * (X_HEAD_DIM ** -0.5)
    p = jax.nn.softmax(sc, axis=-1)
    o = jnp.einsum('bhsm,bhmd->bhsd', p.astype(v.dtype), v)
    return merge_heads(o) @ w_xo


def hier_moe(hn, w_group, b_group, w_expert, b_expert, w_gate, w_up, w_down):
    b, s, dm = hn.shape
    t = hn.reshape(b * s, dm)
    pg = jax.nn.softmax((t @ w_group + b_group).astype(jnp.float32), axis=-1)
    gp, gi = lax.top_k(pg, 1)
    le = (t @ w_expert + b_expert).astype(jnp.float32).reshape(-1, N_GROUPS, EXPERTS_PER_GROUP)
    le = jnp.take_along_axis(le, gi[:, :, None], axis=1)[:, 0]
    pe = jax.nn.softmax(le, axis=-1)
    ew, ei = lax.top_k(pe, TOP_K_IN_GROUP)
    w = gp * ew / jnp.sum(ew, axis=-1, keepdims=True)
    eid = gi * EXPERTS_PER_GROUP + ei
    combine = jnp.sum(jax.nn.one_hot(eid, N_EXPERTS, dtype=jnp.float32) * w[..., None], axis=1).astype(t.dtype)
    out = jnp.zeros_like(t)
    for e in range(N_EXPERTS):
        he = jax.nn.silu(t @ w_gate[e]) * (t @ w_up[e])
        out = out + combine[:, e:e + 1] * (he @ w_down[e])
    return out.reshape(b, s, dm)


def setup_inputs(seed: int = 0) -> dict:
    key = jax.random.key(seed)
    keys = iter(jax.random.split(key, 40))

    def nrm(shape, scale):
        return jax.random.normal(next(keys), shape, jnp.float32) * scale

    def gain(shape):
        return 1.0 + nrm(shape, 0.02)

    L = DEPTH
    x = nrm((BATCH, SEQ, D_MODEL), 1.0)
    mem = nrm((BATCH, N_MEM, D_MODEL), 1.0)
    positions = (jnp.arange(SEQ, dtype=jnp.int32)[None, :]
                 + jax.random.randint(next(keys), (BATCH, 1), 0, MAX_POS_OFFSET, dtype=jnp.int32))
    return {
        'x': x,
        'mem': mem,
        'positions': positions,
        'g_mix': gain((L, D_MODEL)),
        'w_in': nrm((L, D_MODEL, IN_WIDTH), D_MODEL ** -0.5),
        'sinks_a': nrm((L, A_HEADS), 0.5),
        'cmp_pos_k': nrm((L, CMP_LEN, B_HEAD_DIM), 0.1),
        'cmp_pos_v': nrm((L, CMP_LEN, B_HEAD_DIM), 0.1),
        'phi_k1': nrm((L, CMP_LEN * B_HEAD_DIM, B_HEAD_DIM), (CMP_LEN * B_HEAD_DIM) ** -0.5),
        'phi_k2': nrm((L, B_HEAD_DIM, B_HEAD_DIM), B_HEAD_DIM ** -0.5),
        'phi_v1': nrm((L, CMP_LEN * B_HEAD_DIM, B_HEAD_DIM), (CMP_LEN * B_HEAD_DIM) ** -0.5),
        'phi_v2': nrm((L, B_HEAD_DIM, B_HEAD_DIM), B_HEAD_DIM ** -0.5),
        'lq1': nrm((L, C_HEAD_DIM), 0.1),
        'lk1': nrm((L, C_HEAD_DIM), 0.1),
        'lq2': nrm((L, C_HEAD_DIM), 0.1),
        'lk2': nrm((L, C_HEAD_DIM), 0.1),
        'g_diff': gain((L, 2 * C_HEAD_DIM)),
        'w_pa': nrm((L, A_Q_W, D_MODEL), A_Q_W ** -0.5),
        'w_pb': nrm((L, B_Q_W, D_MODEL), B_Q_W ** -0.5),
        'w_pc': nrm((L, C_V_W, D_MODEL), C_V_W ** -0.5),
        'w_out': nrm((L, D_MODEL, D_MODEL), D_MODEL ** -0.5),
        'g_x': gain((L, D_MODEL)),
        'g_mem': gain((L, D_MODEL)),
        'w_xq': nrm((L, D_MODEL, X_WIDTH), D_MODEL ** -0.5),
        'w_xkv': nrm((L, D_MODEL, 2 * X_WIDTH), D_MODEL ** -0.5),
        'w_xo': nrm((L, X_WIDTH, D_MODEL), X_WIDTH ** -0.5),
        'g_ffn': gain((L, D_MODEL)),
        'w_group': nrm((L, D_MODEL, N_GROUPS), D_MODEL ** -0.5),
        'b_group': nrm((L, N_GROUPS), 0.01),
        'w_expert': nrm((L, D_MODEL, N_EXPERTS), D_MODEL ** -0.5),
        'b_expert': nrm((L, N_EXPERTS), 0.01),
        'w_gate': nrm((L, N_EXPERTS, D_MODEL, D_FF_EXPERT), D_MODEL ** -0.5),
        'w_up': nrm((L, N_EXPERTS, D_MODEL, D_FF_EXPERT), D_MODEL ** -0.5),
        'w_down': nrm((L, N_EXPERTS, D_FF_EXPERT, D_MODEL), D_FF_EXPERT ** -0.5),
        'g_final': gain((D_MODEL,)),
    }


def reference(x, mem, positions, g_mix, w_in, sinks_a, cmp_pos_k, cmp_pos_v,
              phi_k1, phi_k2, phi_v1, phi_v2, lq1, lk1, lq2, lk2, g_diff,
              w_pa, w_pb, w_pc, w_out, g_x, g_mem, w_xq, w_xkv, w_xo,
              g_ffn, w_group, b_group, w_expert, b_expert, w_gate, w_up, w_down, g_final):
    split_pts = _split_points()
    for l in range(DEPTH):
        lambda_init = 0.8 - 0.6 * math.exp(-0.3 * l)
        h = rms_norm(x, g_mix[l])
        proj = h @ w_in[l]
        (qa, ka, va, qb, kcb, vcb, ksb, vsb, kwb, vwb, gate_b,
         qc, kc, vc, merge_logits) = jnp.split(proj, split_pts, axis=-1)
        o_a = swa_sink_mixer(qa, ka, va, sinks_a[l], positions)
        o_b = nsa_mixer(qb, kcb, vcb, ksb, vsb, kwb, vwb, gate_b, positions,
                        cmp_pos_k[l], cmp_pos_v[l], phi_k1[l], phi_k2[l], phi_v1[l], phi_v2[l])
        o_c = diff_mixer(qc, kc, vc, positions, lq1[l], lk1[l], lq2[l], lk2[l], g_diff[l], lambda_init)
        gates = jax.nn.sigmoid(merge_logits.astype(jnp.float32)).astype(x.dtype)
        gate_a, gate_bb, gate_c = jnp.split(gates, 3, axis=-1)
        merged = (gate_a * (o_a @ w_pa[l]) + gate_bb * (o_b @ w_pb[l]) + gate_c * (o_c @ w_pc[l]))
        x = x + merged @ w_out[l]
        x = x + memory_cross_attention(rms_norm(x, g_x[l]), rms_norm(mem, g_mem[l]),
                                       w_xq[l], w_xkv[l], w_xo[l])
        x = x + hier_moe(rms_norm(x, g_ffn[l]), w_group[l], b_group[l], w_expert[l], b_expert[l],
                         w_gate[l], w_up[l], w_down[l])
    return rms_norm(x, g_final)
```

```python
import functools
import math

import jax
import jax.numpy as jnp
from jax import lax
from jax.experimental import pallas as pl
from jax.experimental.pallas import tpu as pltpu

D_MODEL = 2048
DEPTH = 2
ROPE_THETA = 10000.0
NORM_EPS = 1e-6
BLOCK_Q = 128
NEG_INF = -1e30
FORCE_SCORE = 1e6
NSA_SLC_CHUNK = 256
DIFF_Q_BLOCK = 256
DIFF_K_CHUNK = 512
MOE_TILE = 256

A_HEADS, A_KV_HEADS, A_HEAD_DIM, A_WINDOW = 8, 2, 64, 128
B_HEADS, B_KV_GROUPS, B_HEAD_DIM = 8, 2, 128
CMP_LEN, CMP_STRIDE, SLC_LEN, SLC_TOPN, B_WINDOW = 32, 16, 64, 8, 512
C_HEADS, C_HEAD_DIM = 4, 64
X_HEADS, X_HEAD_DIM = 4, 128
X_WIDTH = X_HEADS * X_HEAD_DIM
N_GROUPS, EXPERTS_PER_GROUP = 4, 4
N_EXPERTS = N_GROUPS * EXPERTS_PER_GROUP
D_FF_EXPERT = 1024

LANES = 128
VMEM_LIMIT = 52 * 1024 * 1024

_A_Q, _A_K, _A_V = 0, 512, 640
_B_Q, _B_KC, _B_VC, _B_KS, _B_VS, _B_KW, _B_VW, _B_GATE = 768, 1792, 2048, 2304, 2560, 2816, 3072, 3328
_C_Q, _C_K, _C_V, _MERGE, _IN_END = 3352, 3864, 4376, 4888, 11032

R64_AQ, R64_AK, R64_CQ, R64_CK, R64_TILES = 0, 4, 5, 9, 13
R128_BQ, R128_KC, R128_KS, R128_KW, R128_TILES = 0, 8, 10, 12, 14
PL_AV, PL_VC, PL_VS, PL_VW, PL_GATE, PL_CV, PL_TILES = 0, 1, 3, 5, 7, 9, 13

BF16 = jnp.bfloat16
F32 = jnp.float32


def _cparams(sem):
    return pltpu.CompilerParams(dimension_semantics=sem, vmem_limit_bytes=VMEM_LIMIT)


def _dot(a, b):
    return jnp.dot(a, b, preferred_element_type=F32)


def _dot_nt(a, b):
    return lax.dot_general(a, b, (((1,), (1,)), ((), ())), preferred_element_type=F32)


def _sigmoid(x):
    return 1.0 / (1.0 + jnp.exp(-x))


def _rope_table_kernel(pos_ref, inv_ref, sign_ref, cos_ref, sin_ref):
    ang = pos_ref[...] * inv_ref[...]
    cos_ref[...] = jnp.cos(ang)
    sin_ref[...] = jnp.sin(ang) * sign_ref[...]


def rope_tables(pos_f, head_dim):
    n = pos_f.shape[0]
    half = head_dim // 2
    lane = jnp.arange(LANES)
    inv = jnp.power(ROPE_THETA, -(2.0 * (lane % half).astype(F32)) / head_dim)[None, :]
    sign = jnp.where((lane % head_dim) < half, -1.0, 1.0).astype(F32)[None, :]
    tm = 2048
    return pl.pallas_call(
        _rope_table_kernel,
        grid=(n // tm,),
        in_specs=[pl.BlockSpec((tm, 1), lambda i: (i, 0)),
                  pl.BlockSpec((1, LANES), lambda i: (0, 0)),
                  pl.BlockSpec((1, LANES), lambda i: (0, 0))],
        out_specs=[pl.BlockSpec((tm, LANES), lambda i: (i, 0)),
                   pl.BlockSpec((tm, LANES), lambda i: (i, 0))],
        out_shape=[jax.ShapeDtypeStruct((n, LANES), F32)] * 2,
        compiler_params=_cparams(("parallel",)),
        name="rope_tables",
    )(pos_f, inv, sign)


def _norm_proj_kernel(*refs, rope, sigmoid_out):
    if rope:
        x_ref, g_ref, w_ref, cos_ref, sin_ref, o_ref, h_scr = refs
    else:
        x_ref, g_ref, w_ref, o_ref, h_scr = refs

    @pl.when(pl.program_id(1) == 0)
    def _():
        x = x_ref[...]
        ms = jnp.mean(x * x, axis=-1, keepdims=True)
        h_scr[...] = (x * lax.rsqrt(ms + NORM_EPS) * g_ref[...]).astype(BF16)

    acc = _dot(h_scr[...], w_ref[...])
    if rope:
        cos = cos_ref[...]
        sin = sin_ref[...]
        if rope == 64:
            first_half = (lax.broadcasted_iota(jnp.int32, cos.shape, 1) & 63) < 32
    for c in range(o_ref.shape[0]):
        a = acc[:, c * LANES:(c + 1) * LANES]
        if rope == 128:
            a = a * cos + pltpu.roll(a, 64, 1) * sin
        elif rope == 64:
            partner = jnp.where(first_half, pltpu.roll(a, 96, 1), pltpu.roll(a, 32, 1))
            a = a * cos + partner * sin
        if sigmoid_out:
            a = _sigmoid(a)
        o_ref[c] = a.astype(o_ref.dtype)


def norm_proj(x, g, w, *, tm, tn, rope=0, tables=None, sigmoid_out=False):
    n, d = x.shape
    tiles = w.shape[1] // LANES
    tpb = tn // LANES
    in_specs = [pl.BlockSpec((tm, d), lambda i, j: (i, 0)),
                pl.BlockSpec((1, d), lambda i, j: (0, 0)),
                pl.BlockSpec((d, tn), lambda i, j: (0, j))]
    args = [x, g, w]
    if rope:
        in_specs += [pl.BlockSpec((tm, LANES), lambda i, j: (i, 0))] * 2
        args += list(tables)
    return pl.pallas_call(
        functools.partial(_norm_proj_kernel, rope=rope, sigmoid_out=sigmoid_out),
        grid=(n // tm, tiles // tpb),
        in_specs=in_specs,
        out_specs=pl.BlockSpec((tpb, tm, LANES), lambda i, j: (j, i, 0)),
        out_shape=jax.ShapeDtypeStruct((tiles, n, LANES), BF16),
        scratch_shapes=[pltpu.VMEM((tm, d), BF16)],
        compiler_params=_cparams(("parallel", "arbitrary")),
        name="norm_proj",
    )(*args)


def _swa_kernel(sink_ref, q_ref, k_ref, v_ref, o_ref):
    n = pl.program_id(1)
    span = 2 * BLOCK_Q
    start = pl.multiple_of(jnp.maximum(n - 1, 0) * BLOCK_Q, BLOCK_Q)
    kk = k_ref[0, pl.ds(start, span), :]
    vv = v_ref[0, pl.ds(start, span), :]
    qpos = n * BLOCK_Q + lax.broadcasted_iota(jnp.int32, (BLOCK_Q, span), 0)
    kpos = start + lax.broadcasted_iota(jnp.int32, (BLOCK_Q, span), 1)
    rel = qpos - kpos
    scale = A_HEAD_DIM ** -0.5
    heads_per_kv = A_HEADS // A_KV_HEADS
    for t in range(A_HEADS // 2):
        qt = q_ref[t]
        outs = []
        for hh in range(2):
            h = 2 * t + hh
            g = h // heads_per_kv
            qh = qt[:, hh * 64:(hh + 1) * 64]
            kh = kk[:, g * 64:(g + 1) * 64]
            vh = vv[:, g * 64:(g + 1) * 64]
            s = _dot_nt(qh, kh) * scale
            s = jnp.where(rel >= 0, jnp.where(rel < A_WINDOW, s, NEG_INF), NEG_INF)
            sk = sink_ref[h]
            m = jnp.maximum(jnp.max(s, axis=-1, keepdims=True), sk)
            e = jnp.exp(s - m)
            denom = jnp.sum(e, axis=-1, keepdims=True) + jnp.exp(sk - m)
            outs.append(_dot(e.astype(BF16), vh) / denom)
        o_ref[:, t * LANES:(t + 1) * LANES] = jnp.concatenate(outs, axis=-1).astype(o_ref.dtype)


def swa_sink_attention(p64, ppl, sinks, batch, seq):
    n = batch * seq
    nb = seq // BLOCK_Q
    return pl.pallas_call(
        _swa_kernel,
        grid=(batch, nb),
        in_specs=[pl.BlockSpec(memory_space=pltpu.SMEM),
                  pl.BlockSpec((A_HEADS // 2, BLOCK_Q, LANES), lambda b, i: (R64_AQ // 4, b * nb + i, 0)),
                  pl.BlockSpec((1, seq, LANES), lambda b, i: (R64_AK, b, 0)),
                  pl.BlockSpec((1, seq, LANES), lambda b, i: (PL_AV, b, 0))],
        out_specs=pl.BlockSpec((BLOCK_Q, A_HEADS * A_HEAD_DIM), lambda b, i: (b * nb + i, 0)),
        out_shape=jax.ShapeDtypeStruct((n, A_HEADS * A_HEAD_DIM), BF16),
        compiler_params=_cparams(("parallel", "arbitrary")),
        name="swa_sink",
    )(sinks, p64, p64, ppl)


def _compress_kernel(t_ref, pe_ref, w1_ref, w2_ref, o_ref):
    t = t_ref[0, 0]
    half = t.shape[1]
    lo = _dot(t, w1_ref[0:half, :])
    hi = _dot(t, w1_ref[half:2 * half, :])
    rows = t.shape[0]
    hi = pltpu.roll(hi, rows - 1, 0)
    pe = jnp.broadcast_to(pe_ref[...], (8, 2 * half))
    pc = _dot(pe, w1_ref[...])[0:1, :]
    hid = lo + hi + pc
    hid = hid * _sigmoid(hid)
    o_ref[0, 0] = _dot(hid.astype(BF16), w2_ref[...]).astype(o_ref.dtype)


def compress_blocks(tiles, pos_emb, w1, w2, batch, seq):
    g = tiles.shape[0]
    rows = seq // CMP_STRIDE
    t2 = tiles.reshape(g, batch, rows, CMP_STRIDE * LANES)
    pe = pos_emb.reshape(1, CMP_LEN * LANES).astype(BF16)
    return pl.pallas_call(
        _compress_kernel,
        grid=(g, batch),
        in_specs=[pl.BlockSpec((1, 1, rows, CMP_STRIDE * LANES), lambda i, b: (i, b, 0, 0)),
                  pl.BlockSpec((1, CMP_LEN * LANES), lambda i, b: (0, 0)),
                  pl.BlockSpec((CMP_LEN * LANES, LANES), lambda i, b: (0, 0)),
                  pl.BlockSpec((LANES, LANES), lambda i, b: (0, 0))],
        out_specs=pl.BlockSpec((1, 1, rows, LANES), lambda i, b: (i, b, 0, 0)),
        out_shape=jax.ShapeDtypeStruct((g, batch, rows, LANES), BF16),
        compiler_params=_cparams(("parallel", "parallel")),
        name="compress_blocks",
    )(t2, pe, w1.astype(BF16), w2.astype(BF16))


def _nsa_kernel(q_ref, kc_ref, vc_ref, ks_ref, vs_ref, kw_ref, vw_ref, gate_ref, ovlt_ref, o_ref,
                s_scr, mx_scr, ls_scr, acc_scr, *, seq):
    n = pl.program_id(2)
    r = B_HEADS // B_KV_GROUPS
    rq = r * BLOCK_Q
    scale = B_HEAD_DIM ** -0.5
    q4 = q_ref[...].reshape(rq, LANES)
    tpos = n * BLOCK_Q + lax.broadcasted_iota(jnp.int32, (BLOCK_Q, LANES), 0)
    lane = lax.broadcasted_iota(jnp.int32, (BLOCK_Q, LANES), 1)

    nc = seq // CMP_STRIDE - 1
    s = (_dot_nt(q4, kc_ref[0, 0]) * scale).reshape(r, BLOCK_Q, LANES)
    ok = (tpos >= lane * CMP_STRIDE + (CMP_LEN - 1)) & (lane < nc)
    s = jnp.where(ok[None], s, NEG_INF)
    m = jnp.max(s, axis=-1, keepdims=True)
    e = jnp.where(ok[None], jnp.exp(s - m), 0.0)
    l = jnp.sum(e, axis=-1, keepdims=True)
    p = e / jnp.where(l > 0.0, l, 1.0)
    o_cmp = _dot(p.reshape(rq, LANES).astype(BF16), vc_ref[0, 0])

    psum = p[0] + p[1] + p[2] + p[3]
    p_hi = psum.astype(BF16)
    p_mid = (psum - p_hi.astype(F32)).astype(BF16)
    p_lo = (psum - p_hi.astype(F32) - p_mid.astype(F32)).astype(BF16)
    ns = seq // SLC_LEN
    ovl_t = ovlt_ref[...]
    imp_t = (_dot_nt(ovl_t, p_hi) + _dot_nt(ovl_t, p_mid) + _dot_nt(ovl_t, p_lo))[0:ns]
    blk = lax.broadcasted_iota(jnp.int32, (ns, BLOCK_Q), 0)
    tq = n * BLOCK_Q + lax.broadcasted_iota(jnp.int32, (ns, BLOCK_Q), 1)
    cur = tq >> 6
    forced = (blk == 0) | (blk == cur) | (blk == cur - 1)
    future = blk * SLC_LEN > tq
    key = jnp.where(future, -1.0, jnp.where(forced, FORCE_SCORE, imp_t))
    rank = jnp.zeros((ns, BLOCK_Q), F32)
    for i in range(ns):
        ki = key[i:i + 1, :]
        rank = rank + jnp.where(blk > i, jnp.where(ki >= key, 1.0, 0.0), jnp.where(ki > key, 1.0, 0.0))
    sel_t = jnp.where(rank < float(min(SLC_TOPN, ns)), 1.0, 0.0)
    sel_t = jnp.concatenate([sel_t, jnp.zeros((LANES - ns, BLOCK_Q), F32)], axis=0)
    selm = sel_t.T.astype(BF16)

    chunk = s_scr.shape[2]
    halves = chunk // LANES
    mx_scr[...] = jnp.full(mx_scr.shape, NEG_INF, F32)
    ls_scr[...] = jnp.zeros(ls_scr.shape, F32)
    acc_scr[...] = jnp.zeros(acc_scr.shape, F32)
    blk_of_key = lax.broadcasted_iota(jnp.int32, (LANES, chunk), 1) >> 6
    blk_row = lax.broadcasted_iota(jnp.int32, (LANES, chunk), 0)
    kcol = lax.broadcasted_iota(jnp.int32, (BLOCK_Q, chunk), 1)
    trow = n * BLOCK_Q + lax.broadcasted_iota(jnp.int32, (BLOCK_Q, chunk), 0)
    trips = n // (chunk // BLOCK_Q) + 1

    def slc_scores(c, carry):
        base = pl.multiple_of(c * chunk, chunk)
        kk = ks_ref[0, pl.ds(base, chunk), :]
        expand = jnp.where(blk_row == blk_of_key + c * (chunk // SLC_LEN), 1.0, 0.0).astype(BF16)
        keep = jnp.where((base + kcol) <= trow, _dot(selm, expand), 0.0) > 0.5
        sc = (_dot_nt(q4, kk) * scale).reshape(r, BLOCK_Q, chunk)
        sc = jnp.where(keep[None], sc, NEG_INF).reshape(rq, chunk)
        s_scr[c] = sc
        mx = mx_scr[...]
        for j in range(halves):
            mx = jnp.maximum(mx, sc[:, j * LANES:(j + 1) * LANES])
        mx_scr[...] = mx
        return carry

    lax.fori_loop(0, trips, slc_scores, 0)
    mx_scr[...] = jnp.broadcast_to(jnp.max(mx_scr[...], axis=-1, keepdims=True), mx_scr.shape)

    def slc_values(c, carry):
        base = pl.multiple_of(c * chunk, chunk)
        vv = vs_ref[0, pl.ds(base, chunk), :]
        sc = s_scr[c]
        mb = mx_scr[...]
        es = [jnp.exp(sc[:, j * LANES:(j + 1) * LANES] - mb) for j in range(halves)]
        ls_scr[...] = ls_scr[...] + functools.reduce(lambda a, b: a + b, es)
        acc_scr[...] = acc_scr[...] + _dot(jnp.concatenate(es, axis=-1).astype(BF16), vv)
        return carry

    lax.fori_loop(0, trips, slc_values, 0)
    o_slc = acc_scr[...] / jnp.sum(ls_scr[...], axis=-1, keepdims=True)

    span = B_WINDOW + BLOCK_Q
    start = pl.multiple_of(jnp.maximum(n - B_WINDOW // BLOCK_Q, 0) * BLOCK_Q, BLOCK_Q)
    kk = kw_ref[0, pl.ds(start, span), :]
    vv = vw_ref[0, pl.ds(start, span), :]
    rel = (n * BLOCK_Q + lax.broadcasted_iota(jnp.int32, (BLOCK_Q, span), 0)
           - start - lax.broadcasted_iota(jnp.int32, (BLOCK_Q, span), 1))
    sw = (_dot_nt(q4, kk) * scale).reshape(r, BLOCK_Q, span)
    sw = jnp.where((rel >= 0)[None], jnp.where((rel < B_WINDOW)[None], sw, NEG_INF), NEG_INF)
    mw = jnp.max(sw, axis=-1, keepdims=True)
    ew = jnp.exp(sw - mw)
    lw = jnp.sum(ew, axis=-1, keepdims=True).reshape(rq, 1)
    o_win = _dot(ew.reshape(rq, span).astype(BF16), vv) / lw

    gates = _sigmoid(gate_ref[0].astype(F32))
    for hh in range(r):
        rows = slice(hh * BLOCK_Q, (hh + 1) * BLOCK_Q)
        o = (gates[:, 3 * hh:3 * hh + 1] * o_cmp[rows]
             + gates[:, 3 * hh + 1:3 * hh + 2] * o_slc[rows]
             + gates[:, 3 * hh + 2:3 * hh + 3] * o_win[rows])
        o_ref[:, hh * LANES:(hh + 1) * LANES] = o.astype(o_ref.dtype)


def _overlap_matrix_t(seq):
    nc = seq // CMP_STRIDE - 1
    ns = seq // SLC_LEN
    j = jnp.arange(LANES)[:, None]
    c = jnp.arange(LANES)[None, :]
    c_start = c * CMP_STRIDE
    hit = (c_start < (j + 1) * SLC_LEN) & (c_start + CMP_LEN > j * SLC_LEN) & (c < nc) & (j < ns)
    return hit.astype(BF16)


def nsa_attention(p128, ppl, kcmp, vcmp, batch, seq):
    n = batch * seq
    nb = seq // BLOCK_Q
    r = B_HEADS // B_KV_GROUPS
    rows = seq // CMP_STRIDE
    tok = lambda b, g, i: b * nb + i
    return pl.pallas_call(
        functools.partial(_nsa_kernel, seq=seq),
        grid=(batch, B_KV_GROUPS, nb),
        in_specs=[pl.BlockSpec((r, BLOCK_Q, LANES), lambda b, g, i: (R128_BQ // r + g, tok(b, g, i), 0)),
                  pl.BlockSpec((1, 1, rows, LANES), lambda b, g, i: (g, b, 0, 0)),
                  pl.BlockSpec((1, 1, rows, LANES), lambda b, g, i: (g, b, 0, 0)),
                  pl.BlockSpec((1, seq, LANES), lambda b, g, i: (R128_KS + g, b, 0)),
                  pl.BlockSpec((1, seq, LANES), lambda b, g, i: (PL_VS + g, b, 0)),
                  pl.BlockSpec((1, seq, LANES), lambda b, g, i: (R128_KW + g, b, 0)),
                  pl.BlockSpec((1, seq, LANES), lambda b, g, i: (PL_VW + g, b, 0)),
                  pl.BlockSpec((1, BLOCK_Q, LANES), lambda b, g, i: (PL_GATE + g, tok(b, g, i), 0)),
                  pl.BlockSpec((LANES, LANES), lambda b, g, i: (0, 0))],
        out_specs=pl.BlockSpec((BLOCK_Q, r * LANES), lambda b, g, i: (tok(b, g, i), g)),
        out_shape=jax.ShapeDtypeStruct((n, B_HEADS * B_HEAD_DIM), BF16),
        scratch_shapes=[pltpu.VMEM((seq // NSA_SLC_CHUNK, r * BLOCK_Q, NSA_SLC_CHUNK), F32),
                        pltpu.VMEM((r * BLOCK_Q, LANES), F32), pltpu.VMEM((r * BLOCK_Q, LANES), F32),
                        pltpu.VMEM((r * BLOCK_Q, LANES), F32)],
        compiler_params=_cparams(("parallel", "parallel", "arbitrary")),
        name="nsa_attention",
    )(p128, kcmp, vcmp, p128, ppl, p128, ppl, ppl, _overlap_matrix_t(seq))


def _diff_kernel(lam_ref, q_ref, k_ref, v_ref, g_ref, o_ref, s_scr, mx_scr, ls_scr, acc_scr, *, lambda_init):
    n = pl.program_id(2)
    qb = q_ref.shape[1]
    chunk = s_scr.shape[2]
    groups = chunk // LANES
    rows = 2 * qb
    q = q_ref[0] * (C_HEAD_DIM ** -0.5)
    lane = lax.broadcasted_iota(jnp.int32, q.shape, 1)
    zero = jnp.zeros_like(q)
    q2 = jnp.concatenate([jnp.where(lane < C_HEAD_DIM, q, zero), jnp.where(lane >= C_HEAD_DIM, q, zero)], axis=0)
    mx_scr[...] = jnp.full(mx_scr.shape, NEG_INF, F32)
    ls_scr[...] = jnp.zeros(ls_scr.shape, F32)
    acc_scr[...] = jnp.zeros(acc_scr.shape, F32)
    kcol = lax.broadcasted_iota(jnp.int32, (rows, chunk), 1)
    trow = n * qb + (lax.broadcasted_iota(jnp.int32, (rows, chunk), 0) & (qb - 1))
    trips = (n * qb + qb + chunk - 1) // chunk

    def scores(c, carry):
        base = pl.multiple_of(c * chunk, chunk)
        sc = _dot_nt(q2, k_ref[0, pl.ds(base, chunk), :])
        sc = jnp.where(base + kcol <= trow, sc, NEG_INF)
        s_scr[c] = sc
        mx = mx_scr[...]
        for j in range(groups):
            mx = jnp.maximum(mx, sc[:, j * LANES:(j + 1) * LANES])
        mx_scr[...] = mx
        return carry

    lax.fori_loop(0, trips, scores, 0)
    mx_scr[...] = jnp.broadcast_to(jnp.max(mx_scr[...], axis=-1, keepdims=True), mx_scr.shape)

    def values(c, carry):
        base = pl.multiple_of(c * chunk, chunk)
        sc = s_scr[c]
        mb = mx_scr[...]
        es = [jnp.exp(sc[:, j * LANES:(j + 1) * LANES] - mb) for j in range(groups)]
        ls_scr[...] = ls_scr[...] + functools.reduce(lambda a, b: a + b, es)
        acc_scr[...] = acc_scr[...] + _dot(jnp.concatenate(es, axis=-1).astype(BF16),
                                           v_ref[0, pl.ds(base, chunk), :])
        return carry

    lax.fori_loop(0, trips, values, 0)
    a = acc_scr[...] / jnp.sum(ls_scr[...], axis=-1, keepdims=True)
    lp = lam_ref[...]
    lam = (jnp.exp(jnp.sum(lp[0:1] * lp[1:2], axis=-1, keepdims=True))
           - jnp.exp(jnp.sum(lp[2:3] * lp[3:4], axis=-1, keepdims=True)) + lambda_init)
    o = a[0:qb] - lam * a[qb:rows]
    ms = jnp.mean(o * o, axis=-1, keepdims=True)
    o = o * lax.rsqrt(ms + NORM_EPS) * g_ref[...] * (1.0 - lambda_init)
    o_ref[...] = o.astype(o_ref.dtype)


def diff_attention(p64, ppl, lam_params, g_sub, lambda_init, batch, seq):
    n = batch * seq
    qb = DIFF_Q_BLOCK
    nb = seq // qb
    return pl.pallas_call(
        functools.partial(_diff_kernel, lambda_init=lambda_init),
        grid=(batch, C_HEADS, nb),
        in_specs=[pl.BlockSpec((4, C_HEAD_DIM), lambda b, h, i: (0, 0)),
                  pl.BlockSpec((1, qb, LANES), lambda b, h, i: (R64_CQ + h, b * nb + i, 0)),
                  pl.BlockSpec((1, seq, LANES), lambda b, h, i: (R64_CK + h, b, 0)),
                  pl.BlockSpec((1, seq, LANES), lambda b, h, i: (PL_CV + h, b, 0)),
                  pl.BlockSpec((1, LANES), lambda b, h, i: (0, 0))],
        out_specs=pl.BlockSpec((qb, LANES), lambda b, h, i: (b * nb + i, h)),
        out_shape=jax.ShapeDtypeStruct((n, C_HEADS * 2 * C_HEAD_DIM), BF16),
        scratch_shapes=[pltpu.VMEM((seq // DIFF_K_CHUNK, 2 * qb, DIFF_K_CHUNK), F32),
                        pltpu.VMEM((2 * qb, LANES), F32), pltpu.VMEM((2 * qb, LANES), F32),
                        pltpu.VMEM((2 * qb, LANES), F32)],
        compiler_params=_cparams(("parallel", "parallel", "arbitrary")),
        name="diff_attention",
    )(lam_params, p64, p64, ppl, g_sub)


def _merge_kernel(oa_ref, ob_ref, oc_ref, wa_ref, wb_ref, wc_ref, ga_ref, gb_ref, gc_ref, o_ref):
    ya = _dot(oa_ref[...], wa_ref[...])
    yb = _dot(ob_ref[...], wb_ref[...])
    yc = _dot(oc_ref[...], wc_ref[...])
    for c in range(ga_ref.shape[0]):
        cols = slice(c * LANES, (c + 1) * LANES)
        o_ref[:, cols] = (ga_ref[c].astype(F32) * ya[:, cols] + gb_ref[c].astype(F32) * yb[:, cols]
                          + gc_ref[c].astype(F32) * yc[:, cols]).astype(o_ref.dtype)


def merge_branches(o_a, o_b, o_c, w_pa, w_pb, w_pc, gates, *, tm=512, tn=512):
    n = o_a.shape[0]
    d = w_pa.shape[1]
    tpb = tn // LANES
    per_branch = d // tn
    act = lambda k: pl.BlockSpec((tm, k), lambda i, j: (i, 0))
    wsp = lambda k: pl.BlockSpec((k, tn), lambda i, j: (0, j))
    gsp = lambda br: pl.BlockSpec((tpb, tm, LANES), lambda i, j: (br * per_branch + j, i, 0))
    return pl.pallas_call(
        _merge_kernel,
        grid=(n // tm, d // tn),
        in_specs=[act(o_a.shape[1]), act(o_b.shape[1]), act(o_c.shape[1]),
                  wsp(w_pa.shape[0]), wsp(w_pb.shape[0]), wsp(w_pc.shape[0]),
                  gsp(0), gsp(1), gsp(2)],
        out_specs=pl.BlockSpec((tm, tn), lambda i, j: (i, j)),
        out_shape=jax.ShapeDtypeStruct((n, d), BF16),
        compiler_params=_cparams(("parallel", "arbitrary")),
        name="merge_branches",
    )(o_a, o_b, o_c, w_pa, w_pb, w_pc, gates, gates, gates)


def _matmul_residual_kernel(a_ref, w_ref, x_ref, o_ref):
    o_ref[...] = x_ref[...] + _dot(a_ref[...], w_ref[...])


def matmul_residual(a, w, x, *, tm=512, tn=1024):
    n, k = a.shape
    d = w.shape[1]
    return pl.pallas_call(
        _matmul_residual_kernel,
        grid=(n // tm, d // tn),
        in_specs=[pl.BlockSpec((tm, k), lambda i, j: (i, 0)),
                  pl.BlockSpec((k, tn), lambda i, j: (0, j)),
                  pl.BlockSpec((tm, tn), lambda i, j: (i, j))],
        out_specs=pl.BlockSpec((tm, tn), lambda i, j: (i, j)),
        out_shape=jax.ShapeDtypeStruct((n, d), F32),
        compiler_params=_cparams(("parallel", "arbitrary")),
        name="matmul_residual",
    )(a, w, x)


def _xattn_router_kernel(x_ref, gx_ref, wq_ref, kv_ref, wo_ref, gf_ref, wr_hi_ref, wr_lo_ref, br_ref,
                         x2_ref, route_ref, counts_ref):
    x = x_ref[...]
    ms = jnp.mean(x * x, axis=-1, keepdims=True)
    h = (x * lax.rsqrt(ms + NORM_EPS) * gx_ref[...]).astype(BF16)
    q = _dot(h, wq_ref[...]).astype(BF16)
    scale = X_HEAD_DIM ** -0.5
    outs = []
    for hd in range(X_HEADS):
        s = _dot_nt(q[:, hd * LANES:(hd + 1) * LANES], kv_ref[hd]) * scale
        m = jnp.max(s, axis=-1, keepdims=True)
        e = jnp.exp(s - m)
        l = jnp.sum(e, axis=-1, keepdims=True)
        outs.append((_dot(e.astype(BF16), kv_ref[X_HEADS + hd]) / l).astype(BF16))
    x2 = x + _dot(jnp.concatenate(outs, axis=-1), wo_ref[...])
    x2_ref[...] = x2

    ms2 = jnp.mean(x2 * x2, axis=-1, keepdims=True)
    hn = x2 * lax.rsqrt(ms2 + NORM_EPS) * gf_ref[...]
    hn_hi = hn.astype(BF16)
    hn_lo = (hn - hn_hi.astype(F32)).astype(BF16)
    logits = (_dot(hn_hi, wr_hi_ref[...]) + _dot(hn_hi, wr_lo_ref[...]) + _dot(hn_lo, wr_hi_ref[...])
              + br_ref[...])

    lane = lax.broadcasted_iota(jnp.int32, logits.shape, 1).astype(F32)
    big = 1e9
    in_g = jnp.where(lane >= N_EXPERTS, jnp.where(lane < N_EXPERTS + N_GROUPS, 1.0, 0.0), 0.0) > 0.5
    lg = jnp.where(in_g, logits, NEG_INF)
    eg = jnp.where(in_g, jnp.exp(lg - jnp.max(lg, axis=-1, keepdims=True)), 0.0)
    pg = eg / jnp.sum(eg, axis=-1, keepdims=True)
    gp = jnp.max(pg, axis=-1, keepdims=True)
    gi = jnp.min(jnp.where(in_g, jnp.where(pg == gp, lane, big), big), axis=-1, keepdims=True) - N_EXPERTS
    first = EXPERTS_PER_GROUP * gi
    in_e = jnp.where(lane >= first, jnp.where(lane < first + EXPERTS_PER_GROUP, 1.0, 0.0), 0.0) > 0.5
    le = jnp.where(in_e, logits, NEG_INF)
    ee = jnp.where(in_e, jnp.exp(le - jnp.max(le, axis=-1, keepdims=True)), 0.0)
    pe = jnp.where(in_e, ee / jnp.sum(ee, axis=-1, keepdims=True), -1.0)
    p1 = jnp.max(pe, axis=-1, keepdims=True)
    i1 = jnp.min(jnp.where(pe == p1, lane, big), axis=-1, keepdims=True)
    pe2 = jnp.where(lane == i1, -1.0, pe)
    p2 = jnp.max(pe2, axis=-1, keepdims=True)
    i2 = jnp.min(jnp.where(pe2 == p2, jnp.where(in_e, jnp.where(lane == i1, big, lane), big), big),
                 axis=-1, keepdims=True)
    tot = p1 + p2
    route_ref[...] = jnp.where(lane == 0.0, i1, jnp.where(lane == 1.0, i2, jnp.where(
        lane == 2.0, gp * p1 / tot, jnp.where(lane == 3.0, gp * p2 / tot, 0.0))))

    first_step = (pl.program_id(0) == 0) & (pl.program_id(1) == 0)

    @pl.when(first_step)
    def _():
        counts_ref[...] = jnp.zeros(counts_ref.shape, F32)

    hits = jnp.where(lane == i1, 1.0, 0.0) + jnp.where(lane == i2, 1.0, 0.0)
    counts_ref[...] = counts_ref[...] + jnp.sum(hits, axis=0, keepdims=True)


def xattn_router(x, g_x, w_xq, kv, w_xo, g_ffn, wr_hi, wr_lo, b_r, batch, seq, n_mem, *, tm=512):
    n, d = x.shape
    per_b = seq // tm
    full = lambda shape: pl.BlockSpec(shape, lambda b, i: (0,) * len(shape))
    row = lambda width: pl.BlockSpec((tm, width), lambda b, i: (b * per_b + i, 0))
    return pl.pallas_call(
        _xattn_router_kernel,
        grid=(batch, per_b),
        in_specs=[row(d), full((1, d)), full((d, X_WIDTH)),
                  pl.BlockSpec((2 * X_HEADS, n_mem, LANES), lambda b, i: (0, b, 0)),
                  full((X_WIDTH, d)), full((1, d)), full((d, LANES)), full((d, LANES)), full((1, LANES))],
        out_specs=[row(d), row(LANES), full((1, LANES))],
        out_shape=[jax.ShapeDtypeStruct((n, d), F32), jax.ShapeDtypeStruct((n, LANES), F32),
                   jax.ShapeDtypeStruct((1, LANES), F32)],
        compiler_params=_cparams(("arbitrary", "arbitrary")),
        name="xattn_router",
    )(x, g_x, w_xq, kv, w_xo, g_ffn, wr_hi, wr_lo, b_r)


def _plan_kernel(route_ref, off_ref, dest_ref, run_scr):
    @pl.when(pl.program_id(0) == 0)
    def _():
        run_scr[...] = jnp.zeros(run_scr.shape, F32)

    tm = route_ref.shape[0]
    r = route_ref[...]
    lane = lax.broadcasted_iota(jnp.int32, r.shape, 1).astype(F32)
    a1 = jnp.where(lane == r[:, 0:1], 1.0, 0.0)
    a2 = jnp.where(lane == r[:, 1:2], 1.0, 0.0)
    hits = a1 + a2
    earlier = (lax.broadcasted_iota(jnp.int32, (tm, tm), 0) > lax.broadcasted_iota(jnp.int32, (tm, tm), 1))
    before = _dot(jnp.where(earlier, 1.0, 0.0).astype(BF16), hits.astype(BF16))
    slot = off_ref[...] + run_scr[...] + before
    d1 = jnp.sum(a1 * slot, axis=-1, keepdims=True)
    d2 = jnp.sum(a2 * slot, axis=-1, keepdims=True)
    dest_ref[...] = jnp.where(lane == 0.0, d1, jnp.where(lane == 1.0, d2, 0.0)).astype(jnp.int32)
    run_scr[...] = run_scr[...] + jnp.sum(hits, axis=0, keepdims=True)


def dispatch_plan(route, offsets, *, tm=512):
    n = route.shape[0]
    return pl.pallas_call(
        _plan_kernel,
        grid=(n // tm,),
        in_specs=[pl.BlockSpec((tm, LANES), lambda i: (i, 0)), pl.BlockSpec((1, LANES), lambda i: (0, 0))],
        out_specs=pl.BlockSpec((tm, LANES), lambda i: (i, 0)),
        out_shape=jax.ShapeDtypeStruct((n, LANES), jnp.int32),
        scratch_shapes=[pltpu.VMEM((1, LANES), F32)],
        compiler_params=_cparams(("arbitrary",)),
        name="dispatch_plan",
    )(route, offsets)


def _row_copy(src, src_row, dst, dst_row, sem):
    return pltpu.make_async_copy(src.at[pl.ds(src_row, 1), :], dst.at[pl.ds(dst_row, 1), :], sem)


def _dispatch_kernel(zrow_ref, zflag_ref, d1_ref, d2_ref, x_ref, xs_ref, zero_scr, sems):
    tm = x_ref.shape[0]
    tile = zero_scr.shape[0]

    @pl.when(pl.program_id(0) == 0)
    def _():
        zero_scr[...] = jnp.zeros(zero_scr.shape, F32)
        fills = [pltpu.make_async_copy(zero_scr, xs_ref.at[pl.ds(pl.multiple_of(zrow_ref[e], 8), tile), :],
                                       sems.at[2]) for e in range(zrow_ref.shape[0])]
        for e, fill in enumerate(fills):
            pl.when(zflag_ref[e] > 0)(fill.start)
        for e, fill in enumerate(fills):
            pl.when(zflag_ref[e] > 0)(fill.wait)

    def issue(r, carry):
        _row_copy(x_ref, r, xs_ref, d1_ref[r], sems.at[0]).start()
        _row_copy(x_ref, r, xs_ref, d2_ref[r], sems.at[1]).start()
        return carry

    lax.fori_loop(0, tm, issue, 0, unroll=8)
    pltpu.make_async_copy(x_ref, xs_ref.at[pl.ds(0, tm), :], sems.at[0]).wait()
    pltpu.make_async_copy(x_ref, xs_ref.at[pl.ds(0, tm), :], sems.at[1]).wait()


def dispatch_rows(x, d1, d2, zrows, zflags, rows_total, tile, *, tm=512):
    n, d = x.shape
    return pl.pallas_call(
        _dispatch_kernel,
        grid_spec=pltpu.PrefetchScalarGridSpec(
            num_scalar_prefetch=2,
            grid=(n // tm,),
            in_specs=[pl.BlockSpec((tm,), lambda i, z, f: (i,), memory_space=pltpu.SMEM),
                      pl.BlockSpec((tm,), lambda i, z, f: (i,), memory_space=pltpu.SMEM),
                      pl.BlockSpec((tm, d), lambda i, z, f: (i, 0))],
            out_specs=pl.BlockSpec(memory_space=pl.ANY),
            scratch_shapes=[pltpu.VMEM((tile, d), F32), pltpu.SemaphoreType.DMA((3,))]),
        out_shape=jax.ShapeDtypeStruct((rows_total, d), F32),
        compiler_params=_cparams(("arbitrary",)),
        name="dispatch_rows",
    )(zrows, zflags, d1, d2, x)


def _experts_kernel(te_ref, na_ref, xs_ref, g_ref, wg_ref, wu_ref, wd_ref, y_ref):
    active = pl.program_id(0) < na_ref[0]

    @pl.when(active)
    def _():
        x = xs_ref[...]
        ms = jnp.mean(x * x, axis=-1, keepdims=True)
        hn = (x * lax.rsqrt(ms + NORM_EPS) * g_ref[...]).astype(BF16)
        gate = _dot(hn, wg_ref[0])
        up = _dot(hn, wu_ref[0])
        hid = (gate * _sigmoid(gate) * up).astype(BF16)
        y_ref[...] = _dot(hid, wd_ref[0])

    @pl.when(jnp.logical_not(active))
    def _():
        y_ref[...] = jnp.zeros(y_ref.shape, F32)


def expert_ffn(xs, g_ffn, w_gate, w_up, w_down, tile_expert, n_active, tile):
    rows_total, d = xs.shape
    _, _, dff = w_gate.shape
    last = lambda i, te, na: jnp.minimum(i, na[0] - 1)
    return pl.pallas_call(
        _experts_kernel,
        grid_spec=pltpu.PrefetchScalarGridSpec(
            num_scalar_prefetch=2,
            grid=(rows_total // tile,),
            in_specs=[pl.BlockSpec((tile, d), lambda i, te, na: (last(i, te, na), 0)),
                      pl.BlockSpec((1, d), lambda i, te, na: (0, 0)),
                      pl.BlockSpec((1, d, dff), lambda i, te, na: (te[i], 0, 0)),
                      pl.BlockSpec((1, d, dff), lambda i, te, na: (te[i], 0, 0)),
                      pl.BlockSpec((1, dff, d), lambda i, te, na: (te[i], 0, 0))],
            out_specs=pl.BlockSpec((tile, d), lambda i, te, na: (i, 0))),
        out_shape=jax.ShapeDtypeStruct((rows_total, d), F32),
        compiler_params=_cparams(("arbitrary",)),
        name="expert_ffn",
    )(tile_expert, n_active, xs, g_ffn, w_gate, w_up, w_down)


def _combine_kernel(d1_ref, d2_ref, x_ref, route_ref, y_ref, o_ref, buf, sems):
    tm = x_ref.shape[0]

    def issue(r, carry):
        _row_copy(y_ref, d1_ref[r], buf.at[0], r, sems.at[0]).start()
        _row_copy(y_ref, d2_ref[r], buf.at[1], r, sems.at[1]).start()
        return carry

    lax.fori_loop(0, tm, issue, 0, unroll=8)
    pltpu.make_async_copy(y_ref.at[pl.ds(0, tm), :], buf.at[0], sems.at[0]).wait()
    pltpu.make_async_copy(y_ref.at[pl.ds(0, tm), :], buf.at[1], sems.at[1]).wait()
    route = route_ref[...]
    o_ref[...] = x_ref[...] + (route[:, 2:3] * buf[0] + route[:, 3:4] * buf[1])


def combine_rows(x, route, y, d1, d2, *, tm=256):
    n, d = x.shape
    return pl.pallas_call(
        _combine_kernel,
        grid=(n // tm,),
        in_specs=[pl.BlockSpec((tm,), lambda i: (i,), memory_space=pltpu.SMEM),
                  pl.BlockSpec((tm,), lambda i: (i,), memory_space=pltpu.SMEM),
                  pl.BlockSpec((tm, d), lambda i: (i, 0)),
                  pl.BlockSpec((tm, LANES), lambda i: (i, 0)),
                  pl.BlockSpec(memory_space=pl.ANY)],
        out_specs=pl.BlockSpec((tm, d), lambda i: (i, 0)),
        out_shape=jax.ShapeDtypeStruct((n, d), F32),
        scratch_shapes=[pltpu.VMEM((2, tm, d), F32), pltpu.SemaphoreType.DMA((2,))],
        compiler_params=_cparams(("arbitrary",)),
        name="combine_rows",
    )(d1, d2, x, route, y)


def routed_experts(x, route, counts, g_ffn, w_gate, w_up, w_down, *, tile=MOE_TILE):
    n, d = x.shape
    rows_total = 2 * n + N_EXPERTS * tile
    n_tiles = rows_total // tile
    cnt = counts[0, :N_EXPERTS].astype(jnp.int32)
    tiles = (cnt + tile - 1) // tile
    ends = jnp.cumsum(tiles)
    starts = ends - tiles
    n_active = ends[-1:]
    tile_ids = jnp.minimum(jnp.arange(n_tiles, dtype=jnp.int32), n_active[0] - 1)
    tile_expert = jnp.sum((tile_ids[:, None] >= ends[None, :]).astype(jnp.int32), axis=1)
    offsets = jnp.pad((starts * tile).astype(F32)[None, :], ((0, 0), (0, LANES - N_EXPERTS)))
    tail = n_active[0] + jnp.arange(N_EXPERTS, dtype=jnp.int32)
    zrows = (jnp.concatenate([jnp.maximum(ends - 1, starts), jnp.minimum(tail, n_tiles - 1)]) * tile).astype(jnp.int32)
    zflags = jnp.concatenate([tiles > 0, tail < n_tiles]).astype(jnp.int32)

    dest = dispatch_plan(route, offsets)
    d1, d2 = dest[:, 0], dest[:, 1]
    xs = dispatch_rows(x, d1, d2, zrows, zflags, rows_total, tile)
    y = expert_ffn(xs, g_ffn, w_gate, w_up, w_down, tile_expert.astype(jnp.int32), n_active.astype(jnp.int32), tile)
    return combine_rows(x, route, y, d1, d2)


def _rms_kernel(x_ref, g_ref, o_ref):
    x = x_ref[...]
    ms = jnp.mean(x * x, axis=-1, keepdims=True)
    o_ref[...] = x * lax.rsqrt(ms + NORM_EPS) * g_ref[...]


def rms_norm_rows(x, g, *, tm=512):
    n, d = x.shape
    return pl.pallas_call(
        _rms_kernel,
        grid=(n // tm,),
        in_specs=[pl.BlockSpec((tm, d), lambda i: (i, 0)), pl.BlockSpec((1, d), lambda i: (0, 0))],
        out_specs=pl.BlockSpec((tm, d), lambda i: (i, 0)),
        out_shape=jax.ShapeDtypeStruct((n, d), F32),
        compiler_params=_cparams(("parallel",)),
        name="final_norm",
    )(x, g)


def _pad_cols(w, width):
    return jnp.pad(w, ((0, 0), (0, width - w.shape[1])))


def _input_weights(w):
    cols = lambda lo, hi: w[:, lo:hi]
    w64 = jnp.concatenate([cols(_A_Q, _A_K), cols(_A_K, _A_V), cols(_C_Q, _C_K), cols(_C_K, _C_V)], axis=1)
    w128 = jnp.concatenate([cols(_B_Q, _B_KC), cols(_B_KC, _B_VC), cols(_B_KS, _B_VS), cols(_B_KW, _B_VW)], axis=1)
    per_group = 3 * B_HEADS // B_KV_GROUPS
    gate_tiles = [_pad_cols(cols(_B_GATE + g * per_group, _B_GATE + (g + 1) * per_group), LANES)
                  for g in range(B_KV_GROUPS)]
    wpl = jnp.concatenate([cols(_A_V, _B_Q), cols(_B_VC, _B_KS), cols(_B_VS, _B_KW), cols(_B_VW, _B_GATE)]
                          + gate_tiles + [cols(_C_V, _MERGE)], axis=1)
    wmg = cols(_MERGE, _IN_END)
    return w64.astype(BF16), w128.astype(BF16), wpl.astype(BF16), wmg.astype(BF16)


def kernel(x, mem, positions, g_mix, w_in, sinks_a, cmp_pos_k, cmp_pos_v, phi_k1, phi_k2, phi_v1, phi_v2,
           lq1, lk1, lq2, lk2, g_diff, w_pa, w_pb, w_pc, w_out, g_x, g_mem, w_xq, w_xkv, w_xo,
           g_ffn, w_group, b_group, w_expert, b_expert, w_gate, w_up, w_down, g_final):
    batch, seq, d = x.shape
    n_mem = mem.shape[1]
    n = batch * seq
    xf = x.reshape(n, d)
    memf = mem.reshape(batch * n_mem, d)
    pos_f = positions.reshape(n, 1).astype(F32)
    tab64 = rope_tables(pos_f, A_HEAD_DIM)
    tab128 = rope_tables(pos_f, B_HEAD_DIM)
    row = lambda v: v.reshape(1, -1)

    for l in range(DEPTH):
        lambda_init = 0.8 - 0.6 * math.exp(-0.3 * l)
        w64, w128, wpl, wmg = _input_weights(w_in[l])
        g = row(g_mix[l])
        p64 = norm_proj(xf, g, w64, tm=512, tn=w64.shape[1], rope=64, tables=tab64)
        p128 = norm_proj(xf, g, w128, tm=512, tn=w128.shape[1], rope=128, tables=tab128)
        ppl = norm_proj(xf, g, wpl, tm=512, tn=wpl.shape[1])
        gates = norm_proj(xf, g, wmg, tm=1024, tn=1024, sigmoid_out=True)

        o_a = swa_sink_attention(p64, ppl, sinks_a[l], batch, seq)
        kcmp = compress_blocks(p128[R128_KC:R128_KC + B_KV_GROUPS], cmp_pos_k[l], phi_k1[l], phi_k2[l], batch, seq)
        vcmp = compress_blocks(ppl[PL_VC:PL_VC + B_KV_GROUPS], cmp_pos_v[l], phi_v1[l], phi_v2[l], batch, seq)
        o_b = nsa_attention(p128, ppl, kcmp, vcmp, batch, seq)
        lam_params = jnp.stack([lq1[l], lk1[l], lq2[l], lk2[l]])
        o_c = diff_attention(p64, ppl, lam_params, row(g_diff[l]), lambda_init, batch, seq)

        merged = merge_branches(o_a, o_b, o_c, w_pa[l].astype(BF16), w_pb[l].astype(BF16),
                                w_pc[l].astype(BF16), gates)
        xf = matmul_residual(merged, w_out[l].astype(BF16), xf)

        kv = norm_proj(memf, row(g_mem[l]), w_xkv[l].astype(BF16), tm=512, tn=2 * X_WIDTH)
        w_r = _pad_cols(jnp.concatenate([w_expert[l], w_group[l]], axis=1), LANES)
        wr_hi = w_r.astype(BF16)
        wr_lo = (w_r - wr_hi.astype(F32)).astype(BF16)
        b_r = _pad_cols(jnp.concatenate([b_expert[l], b_group[l]])[None, :], LANES)
        xf, route, counts = xattn_router(xf, row(g_x[l]), w_xq[l].astype(BF16), kv, w_xo[l].astype(BF16),
                                         row(g_ffn[l]), wr_hi, wr_lo, b_r, batch, seq, n_mem)
        xf = routed_experts(xf, route, counts, row(g_ffn[l]), w_gate[l].astype(BF16), w_up[l].astype(BF16),
                            w_down[l].astype(BF16))

    return rms_norm_rows(xf, row(g_final)).reshape(batch, seq, d)
```

```python
import functools
import math

import jax
import jax.numpy as jnp
from jax import lax
from jax.experimental import pallas as pl
from jax.experimental.pallas import tpu as pltpu

D_MODEL = 2048
DEPTH = 2
ROPE_THETA = 10000.0
NORM_EPS = 1e-6
BLOCK_Q = 128
NEG_INF = -1e30
FORCE_SCORE = 1e6
NSA_SLC_CHUNK = 256
DIFF_Q_BLOCK = 256
MOE_TILE = 256

A_HEADS, A_KV_HEADS, A_HEAD_DIM, A_WINDOW = 8, 2, 64, 128
B_HEADS, B_KV_GROUPS, B_HEAD_DIM = 8, 2, 128
CMP_LEN, CMP_STRIDE, SLC_LEN, SLC_TOPN, B_WINDOW = 32, 16, 64, 8, 512
C_HEADS, C_HEAD_DIM = 4, 64
X_HEADS, X_HEAD_DIM = 4, 128
X_WIDTH = X_HEADS * X_HEAD_DIM
N_GROUPS, EXPERTS_PER_GROUP = 4, 4
N_EXPERTS = N_GROUPS * EXPERTS_PER_GROUP
D_FF_EXPERT = 1024

LANES = 128
VMEM_LIMIT = 52 * 1024 * 1024

_A_Q, _A_K, _A_V = 0, 512, 640
_B_Q, _B_KC, _B_VC, _B_KS, _B_VS, _B_KW, _B_VW, _B_GATE = 768, 1792, 2048, 2304, 2560, 2816, 3072, 3328
_C_Q, _C_K, _C_V, _MERGE, _IN_END = 3352, 3864, 4376, 4888, 11032

R64_AQ, R64_AK, R64_CQ, R64_CK, R64_TILES = 0, 4, 5, 9, 13
R128_BQ, R128_KC, R128_KS, R128_KW, R128_TILES = 0, 8, 10, 12, 14
PL_AV, PL_VC, PL_VS, PL_VW, PL_GATE, PL_CV, PL_TILES = 0, 1, 3, 5, 7, 9, 13

BF16 = jnp.bfloat16
F32 = jnp.float32


def _cparams(sem):
    return pltpu.CompilerParams(dimension_semantics=sem, vmem_limit_bytes=VMEM_LIMIT)


def _dot(a, b):
    return jnp.dot(a, b, preferred_element_type=F32)


def _dot_nt(a, b):
    return lax.dot_general(a, b, (((1,), (1,)), ((), ())), preferred_element_type=F32)


def _sigmoid(x):
    return 1.0 / (1.0 + jnp.exp(-x))


def _rope_table_kernel(pos_ref, inv_ref, sign_ref, cos_ref, sin_ref):
    ang = pos_ref[...] * inv_ref[...]
    cos_ref[...] = jnp.cos(ang)
    sin_ref[...] = jnp.sin(ang) * sign_ref[...]


def rope_tables(pos_f, head_dim):
    n = pos_f.shape[0]
    half = head_dim // 2
    lane = jnp.arange(LANES)
    inv = jnp.power(ROPE_THETA, -(2.0 * (lane % half).astype(F32)) / head_dim)[None, :]
    sign = jnp.where((lane % head_dim) < half, -1.0, 1.0).astype(F32)[None, :]
    tm = 2048
    return pl.pallas_call(
        _rope_table_kernel,
        grid=(n // tm,),
        in_specs=[pl.BlockSpec((tm, 1), lambda i: (i, 0)),
                  pl.BlockSpec((1, LANES), lambda i: (0, 0)),
                  pl.BlockSpec((1, LANES), lambda i: (0, 0))],
        out_specs=[pl.BlockSpec((tm, LANES), lambda i: (i, 0)),
                   pl.BlockSpec((tm, LANES), lambda i: (i, 0))],
        out_shape=[jax.ShapeDtypeStruct((n, LANES), F32)] * 2,
        compiler_params=_cparams(("parallel",)),
        name="rope_tables",
    )(pos_f, inv, sign)


def _norm_proj_kernel(*refs, rope, sigmoid_out):
    if rope:
        x_ref, g_ref, w_ref, cos_ref, sin_ref, o_ref, h_scr = refs
    else:
        x_ref, g_ref, w_ref, o_ref, h_scr = refs

    @pl.when(pl.program_id(1) == 0)
    def _():
        x = x_ref[...]
        ms = jnp.mean(x * x, axis=-1, keepdims=True)
        h_scr[...] = (x * lax.rsqrt(ms + NORM_EPS) * g_ref[...]).astype(BF16)

    acc = _dot(h_scr[...], w_ref[...])
    if rope:
        cos = cos_ref[...]
        sin = sin_ref[...]
        if rope == 64:
            first_half = (lax.broadcasted_iota(jnp.int32, cos.shape, 1) & 63) < 32
    for c in range(o_ref.shape[0]):
        a = acc[:, c * LANES:(c + 1) * LANES]
        if rope == 128:
            a = a * cos + pltpu.roll(a, 64, 1) * sin
        elif rope == 64:
            partner = jnp.where(first_half, pltpu.roll(a, 96, 1), pltpu.roll(a, 32, 1))
            a = a * cos + partner * sin
        if sigmoid_out:
            a = _sigmoid(a)
        o_ref[c] = a.astype(o_ref.dtype)


def norm_proj(x, g, w, *, tm, tn, rope=0, tables=None, sigmoid_out=False):
    n, d = x.shape
    tiles = w.shape[1] // LANES
    tpb = tn // LANES
    in_specs = [pl.BlockSpec((tm, d), lambda i, j: (i, 0)),
                pl.BlockSpec((1, d), lambda i, j: (0, 0)),
                pl.BlockSpec((d, tn), lambda i, j: (0, j))]
    args = [x, g, w]
    if rope:
        in_specs += [pl.BlockSpec((tm, LANES), lambda i, j: (i, 0))] * 2
        args += list(tables)
    return pl.pallas_call(
        functools.partial(_norm_proj_kernel, rope=rope, sigmoid_out=sigmoid_out),
        grid=(n // tm, tiles // tpb),
        in_specs=in_specs,
        out_specs=pl.BlockSpec((tpb, tm, LANES), lambda i, j: (j, i, 0)),
        out_shape=jax.ShapeDtypeStruct((tiles, n, LANES), BF16),
        scratch_shapes=[pltpu.VMEM((tm, d), BF16)],
        compiler_params=_cparams(("parallel", "arbitrary")),
        name="norm_proj",
    )(*args)


def _swa_kernel(sink_ref, q_ref, k_ref, v_ref, o_ref):
    n = pl.program_id(1)
    span = 2 * BLOCK_Q
    start = pl.multiple_of(jnp.maximum(n - 1, 0) * BLOCK_Q, BLOCK_Q)
    kk = k_ref[0, pl.ds(start, span), :]
    vv = v_ref[0, pl.ds(start, span), :]
    qpos = n * BLOCK_Q + lax.broadcasted_iota(jnp.int32, (BLOCK_Q, span), 0)
    kpos = start + lax.broadcasted_iota(jnp.int32, (BLOCK_Q, span), 1)
    rel = qpos - kpos
    scale = A_HEAD_DIM ** -0.5
    heads_per_kv = A_HEADS // A_KV_HEADS
    for t in range(A_HEADS // 2):
        qt = q_ref[t]
        outs = []
        for hh in range(2):
            h = 2 * t + hh
            g = h // heads_per_kv
            qh = qt[:, hh * 64:(hh + 1) * 64]
            kh = kk[:, g * 64:(g + 1) * 64]
            vh = vv[:, g * 64:(g + 1) * 64]
            s = _dot_nt(qh, kh) * scale
            s = jnp.where(rel >= 0, jnp.where(rel < A_WINDOW, s, NEG_INF), NEG_INF)
            sk = sink_ref[h]
            m = jnp.maximum(jnp.max(s, axis=-1, keepdims=True), sk)
            e = jnp.exp(s - m)
            denom = jnp.sum(e, axis=-1, keepdims=True) + jnp.exp(sk - m)
            outs.append(_dot(e.astype(BF16), vh) / denom)
        o_ref[:, t * LANES:(t + 1) * LANES] = jnp.concatenate(outs, axis=-1).astype(o_ref.dtype)


def swa_sink_attention(p64, ppl, sinks, batch, seq):
    n = batch * seq
    nb = seq // BLOCK_Q
    return pl.pallas_call(
        _swa_kernel,
        grid=(batch, nb),
        in_specs=[pl.BlockSpec(memory_space=pltpu.SMEM),
                  pl.BlockSpec((A_HEADS // 2, BLOCK_Q, LANES), lambda b, i: (R64_AQ // 4, b * nb + i, 0)),
                  pl.BlockSpec((1, seq, LANES), lambda b, i: (R64_AK, b, 0)),
                  pl.BlockSpec((1, seq, LANES), lambda b, i: (PL_AV, b, 0))],
        out_specs=pl.BlockSpec((BLOCK_Q, A_HEADS * A_HEAD_DIM), lambda b, i: (b * nb + i, 0)),
        out_shape=jax.ShapeDtypeStruct((n, A_HEADS * A_HEAD_DIM), BF16),
        compiler_params=_cparams(("parallel", "arbitrary")),
        name="swa_sink",
    )(sinks, p64, p64, ppl)


def _compress_kernel(t_ref, pe_ref, w1_ref, w2_ref, o_ref):
    t = t_ref[0, 0]
    half = t.shape[1]
    lo = _dot(t, w1_ref[0:half, :])
    hi = _dot(t, w1_ref[half:2 * half, :])
    rows = t.shape[0]
    hi = pltpu.roll(hi, rows - 1, 0)
    pe = jnp.broadcast_to(pe_ref[...], (8, 2 * half))
    pc = _dot(pe, w1_ref[...])[0:1, :]
    hid = lo + hi + pc
    hid = hid * _sigmoid(hid)
    o_ref[0, 0] = _dot(hid.astype(BF16), w2_ref[...]).astype(o_ref.dtype)


def compress_blocks(tiles, pos_emb, w1, w2, batch, seq):
    g = tiles.shape[0]
    rows = seq // CMP_STRIDE
    t2 = tiles.reshape(g, batch, rows, CMP_STRIDE * LANES)
    pe = pos_emb.reshape(1, CMP_LEN * LANES).astype(BF16)
    return pl.pallas_call(
        _compress_kernel,
        grid=(g, batch),
        in_specs=[pl.BlockSpec((1, 1, rows, CMP_STRIDE * LANES), lambda i, b: (i, b, 0, 0)),
                  pl.BlockSpec((1, CMP_LEN * LANES), lambda i, b: (0, 0)),
                  pl.BlockSpec((CMP_LEN * LANES, LANES), lambda i, b: (0, 0)),
                  pl.BlockSpec((LANES, LANES), lambda i, b: (0, 0))],
        out_specs=pl.BlockSpec((1, 1, rows, LANES), lambda i, b: (i, b, 0, 0)),
        out_shape=jax.ShapeDtypeStruct((g, batch, rows, LANES), BF16),
        compiler_params=_cparams(("parallel", "parallel")),
        name="compress_blocks",
    )(t2, pe, w1.astype(BF16), w2.astype(BF16))


def _nsa_kernel(q_ref, kc_ref, vc_ref, ks_ref, vs_ref, kw_ref, vw_ref, gate_ref, ovlt_ref, o_ref,
                s_scr, mx_scr, ls_scr, acc_scr, *, seq):
    n = pl.program_id(2)
    r = B_HEADS // B_KV_GROUPS
    rq = r * BLOCK_Q
    scale = B_HEAD_DIM ** -0.5
    q4 = q_ref[...].reshape(rq, LANES)
    tpos = n * BLOCK_Q + lax.broadcasted_iota(jnp.int32, (BLOCK_Q, LANES), 0)
    lane = lax.broadcasted_iota(jnp.int32, (BLOCK_Q, LANES), 1)

    nc = seq // CMP_STRIDE - 1
    s = (_dot_nt(q4, kc_ref[0, 0]) * scale).reshape(r, BLOCK_Q, LANES)
    ok = (tpos >= lane * CMP_STRIDE + (CMP_LEN - 1)) & (lane < nc)
    s = jnp.where(ok[None], s, NEG_INF)
    m = jnp.max(s, axis=-1, keepdims=True)
    e = jnp.where(ok[None], jnp.exp(s - m), 0.0)
    l = jnp.sum(e, axis=-1, keepdims=True)
    p = e / jnp.where(l > 0.0, l, 1.0)
    o_cmp = _dot(p.reshape(rq, LANES).astype(BF16), vc_ref[0, 0])

    psum = p[0] + p[1] + p[2] + p[3]
    p_hi = psum.astype(BF16)
    p_mid = (psum - p_hi.astype(F32)).astype(BF16)
    p_lo = (psum - p_hi.astype(F32) - p_mid.astype(F32)).astype(BF16)
    ns = seq // SLC_LEN
    ovl_t = ovlt_ref[...]
    imp_t = (_dot_nt(ovl_t, p_hi) + _dot_nt(ovl_t, p_mid) + _dot_nt(ovl_t, p_lo))[0:ns]
    blk = lax.broadcasted_iota(jnp.int32, (ns, BLOCK_Q), 0)
    tq = n * BLOCK_Q + lax.broadcasted_iota(jnp.int32, (ns, BLOCK_Q), 1)
    cur = tq >> 6
    forced = (blk == 0) | (blk == cur) | (blk == cur - 1)
    future = blk * SLC_LEN > tq
    key = jnp.where(future, -1.0, jnp.where(forced, FORCE_SCORE, imp_t))
    rank = jnp.zeros((ns, BLOCK_Q), F32)
    for i in range(ns):
        ki = key[i:i + 1, :]
        rank = rank + jnp.where(blk > i, jnp.where(ki >= key, 1.0, 0.0), jnp.where(ki > key, 1.0, 0.0))
    sel_t = jnp.where(rank < float(min(SLC_TOPN, ns)), 1.0, 0.0)
    sel_t = jnp.concatenate([sel_t, jnp.zeros((LANES - ns, BLOCK_Q), F32)], axis=0)
    selm = sel_t.T.astype(BF16)

    chunk = s_scr.shape[2]
    halves = chunk // LANES
    mx_scr[...] = jnp.full(mx_scr.shape, NEG_INF, F32)
    ls_scr[...] = jnp.zeros(ls_scr.shape, F32)
    acc_scr[...] = jnp.zeros(acc_scr.shape, F32)
    blk_of_key = lax.broadcasted_iota(jnp.int32, (LANES, chunk), 1) >> 6
    blk_row = lax.broadcasted_iota(jnp.int32, (LANES, chunk), 0)
    kcol = lax.broadcasted_iota(jnp.int32, (BLOCK_Q, chunk), 1)
    trow = n * BLOCK_Q + lax.broadcasted_iota(jnp.int32, (BLOCK_Q, chunk), 0)
    trips = n // (chunk // BLOCK_Q) + 1

    def slc_scores(c, carry):
        base = pl.multiple_of(c * chunk, chunk)
        kk = ks_ref[0, pl.ds(base, chunk), :]
        expand = jnp.where(blk_row == blk_of_key + c * (chunk // SLC_LEN), 1.0, 0.0).astype(BF16)
        keep = jnp.where((base + kcol) <= trow, _dot(selm, expand), 0.0) > 0.5
        sc = (_dot_nt(q4, kk) * scale).reshape(r, BLOCK_Q, chunk)
        sc = jnp.where(keep[None], sc, NEG_INF).reshape(rq, chunk)
        s_scr[c] = sc
        mx = mx_scr[...]
        for j in range(halves):
            mx = jnp.maximum(mx, sc[:, j * LANES:(j + 1) * LANES])
        mx_scr[...] = mx
        return carry

    lax.fori_loop(0, trips, slc_scores, 0)
    mx_scr[...] = jnp.broadcast_to(jnp.max(mx_scr[...], axis=-1, keepdims=True), mx_scr.shape)

    def slc_values(c, carry):
        base = pl.multiple_of(c * chunk, chunk)
        vv = vs_ref[0, pl.ds(base, chunk), :]
        sc = s_scr[c]
        mb = mx_scr[...]
        es = [jnp.exp(sc[:, j * LANES:(j + 1) * LANES] - mb) for j in range(halves)]
        ls_scr[...] = ls_scr[...] + functools.reduce(lambda a, b: a + b, es)
        acc_scr[...] = acc_scr[...] + _dot(jnp.concatenate(es, axis=-1).astype(BF16), vv)
        return carry

    lax.fori_loop(0, trips, slc_values, 0)
    o_slc = acc_scr[...] / jnp.sum(ls_scr[...], axis=-1, keepdims=True)

    span = B_WINDOW + BLOCK_Q
    start = pl.multiple_of(jnp.maximum(n - B_WINDOW // BLOCK_Q, 0) * BLOCK_Q, BLOCK_Q)
    kk = kw_ref[0, pl.ds(start, span), :]
    vv = vw_ref[0, pl.ds(start, span), :]
    rel = (n * BLOCK_Q + lax.broadcasted_iota(jnp.int32, (BLOCK_Q, span), 0)
           - start - lax.broadcasted_iota(jnp.int32, (BLOCK_Q, span), 1))
    sw = (_dot_nt(q4, kk) * scale).reshape(r, BLOCK_Q, span)
    sw = jnp.where((rel >= 0)[None], jnp.where((rel < B_WINDOW)[None], sw, NEG_INF), NEG_INF)
    mw = jnp.max(sw, axis=-1, keepdims=True)
    ew = jnp.exp(sw - mw)
    lw = jnp.sum(ew, axis=-1, keepdims=True).reshape(rq, 1)
    o_win = _dot(ew.reshape(rq, span).astype(BF16), vv) / lw

    gates = _sigmoid(gate_ref[0].astype(F32))
    for hh in range(r):
        rows = slice(hh * BLOCK_Q, (hh + 1) * BLOCK_Q)
        o = (gates[:, 3 * hh:3 * hh + 1] * o_cmp[rows]
             + gates[:, 3 * hh + 1:3 * hh + 2] * o_slc[rows]
             + gates[:, 3 * hh + 2:3 * hh + 3] * o_win[rows])
        o_ref[:, hh * LANES:(hh + 1) * LANES] = o.astype(o_ref.dtype)


def _overlap_matrix_t(seq):
    nc = seq // CMP_STRIDE - 1
    ns = seq // SLC_LEN
    j = jnp.arange(LANES)[:, None]
    c = jnp.arange(LANES)[None, :]
    c_start = c * CMP_STRIDE
    hit = (c_start < (j + 1) * SLC_LEN) & (c_start + CMP_LEN > j * SLC_LEN) & (c < nc) & (j < ns)
    return hit.astype(BF16)


def nsa_attention(p128, ppl, kcmp, vcmp, batch, seq):
    n = batch * seq
    nb = seq // BLOCK_Q
    r = B_HEADS // B_KV_GROUPS
    rows = seq // CMP_STRIDE
    tok = lambda b, g, i: b * nb + i
    return pl.pallas_call(
        functools.partial(_nsa_kernel, seq=seq),
        grid=(batch, B_KV_GROUPS, nb),
        in_specs=[pl.BlockSpec((r, BLOCK_Q, LANES), lambda b, g, i: (R128_BQ // r + g, tok(b, g, i), 0)),
                  pl.BlockSpec((1, 1, rows, LANES), lambda b, g, i: (g, b, 0, 0)),
                  pl.BlockSpec((1, 1, rows, LANES), lambda b, g, i: (g, b, 0, 0)),
                  pl.BlockSpec((1, seq, LANES), lambda b, g, i: (R128_KS + g, b, 0)),
                  pl.BlockSpec((1, seq, LANES), lambda b, g, i: (PL_VS + g, b, 0)),
                  pl.BlockSpec((1, seq, LANES), lambda b, g, i: (R128_KW + g, b, 0)),
                  pl.BlockSpec((1, seq, LANES), lambda b, g, i: (PL_VW + g, b, 0)),
                  pl.BlockSpec((1, BLOCK_Q, LANES), lambda b, g, i: (PL_GATE + g, tok(b, g, i), 0)),
                  pl.BlockSpec((LANES, LANES), lambda b, g, i: (0, 0))],
        out_specs=pl.BlockSpec((BLOCK_Q, r * LANES), lambda b, g, i: (tok(b, g, i), g)),
        out_shape=jax.ShapeDtypeStruct((n, B_HEADS * B_HEAD_DIM), BF16),
        scratch_shapes=[pltpu.VMEM((seq // NSA_SLC_CHUNK, r * BLOCK_Q, NSA_SLC_CHUNK), F32),
                        pltpu.VMEM((r * BLOCK_Q, LANES), F32), pltpu.VMEM((r * BLOCK_Q, LANES), F32),
                        pltpu.VMEM((r * BLOCK_Q, LANES), F32)],
        compiler_params=_cparams(("parallel", "parallel", "arbitrary")),
        name="nsa_attention",
    )(p128, kcmp, vcmp, p128, ppl, p128, ppl, ppl, _overlap_matrix_t(seq))


def _diff_block(q_ref, k_ref, v_ref, o_ref, s_scr, slot0, first_query, n_chunks, lam, gain):
    qb = q_ref.shape[1]
    _, rows, chunk = s_scr.shape
    groups = chunk // LANES
    q = q_ref[0] * (C_HEAD_DIM ** -0.5)
    lane = lax.broadcasted_iota(jnp.int32, (qb, LANES), 1)
    zero = jnp.zeros_like(q)
    q2 = jnp.concatenate([jnp.where(lane < C_HEAD_DIM, q, zero), jnp.where(lane >= C_HEAD_DIM, q, zero)], axis=0)

    mx = jnp.full((rows, LANES), NEG_INF, F32)
    for c in range(n_chunks):
        sc = _dot_nt(q2, k_ref[0, c * chunk:(c + 1) * chunk, :])
        if c == n_chunks - 1:
            ahead = (lax.broadcasted_iota(jnp.int32, (rows, chunk), 1)
                     - (lax.broadcasted_iota(jnp.int32, (rows, chunk), 0) & (qb - 1)))
            sc = jnp.where(ahead <= first_query - c * chunk, sc, NEG_INF)
        s_scr[slot0 + c] = sc
        for j in range(groups):
            mx = jnp.maximum(mx, sc[:, j * LANES:(j + 1) * LANES])
    mb = jnp.broadcast_to(jnp.max(mx, axis=-1, keepdims=True), (rows, LANES))

    ls = jnp.zeros((rows, LANES), F32)
    acc = jnp.zeros((rows, LANES), F32)
    for c in range(n_chunks):
        sc = s_scr[slot0 + c]
        es = [jnp.exp(sc[:, j * LANES:(j + 1) * LANES] - mb) for j in range(groups)]
        ls = ls + functools.reduce(lambda a, b: a + b, es)
        acc = acc + _dot(jnp.concatenate(es, axis=-1).astype(BF16), v_ref[0, c * chunk:(c + 1) * chunk, :])
    a = acc / jnp.sum(ls, axis=-1, keepdims=True)
    o = a[0:qb] - lam * a[qb:rows]
    ms = jnp.mean(o * o, axis=-1, keepdims=True)
    o_ref[0, 0] = (o * lax.rsqrt(ms + NORM_EPS) * gain).astype(o_ref.dtype)


def _diff_kernel(lam_ref, qlo_ref, qhi_ref, k_ref, v_ref, g_ref, olo_ref, ohi_ref, s_scr, *, lambda_init):
    i = pl.program_id(2)
    half = pl.num_programs(2)
    qb = qlo_ref.shape[1]
    total, _, chunk = s_scr.shape
    lp = lam_ref[...]
    lam = (jnp.exp(jnp.sum(lp[0:1] * lp[1:2], axis=-1, keepdims=True))
           - jnp.exp(jnp.sum(lp[2:3] * lp[3:4], axis=-1, keepdims=True)) + lambda_init)
    gain = g_ref[...] * (1.0 - lambda_init)
    chunks_lo = lax.shift_right_logical(i * qb + qb + chunk - 1, chunk.bit_length() - 1)
    for n_lo in range(1, (half * qb + chunk - 1) // chunk + 1):
        @pl.when(chunks_lo == n_lo)
        def _():
            _diff_block(qlo_ref, k_ref, v_ref, olo_ref, s_scr, 0, i * qb, n_lo, lam, gain)
            _diff_block(qhi_ref, k_ref, v_ref, ohi_ref, s_scr, n_lo, (2 * half - 1 - i) * qb, total - n_lo,
                        lam, gain)


def _pair_blocks(lo, hi, n, width):
    return jnp.concatenate([lo, jnp.flip(hi, axis=1)], axis=1).reshape(n, width)


def diff_attention(p64, ppl, lam_params, g_sub, lambda_init, batch, seq):
    n = batch * seq
    qb = DIFF_Q_BLOCK
    chunk = 2 * qb
    nb = seq // qb
    half = nb // 2
    width = C_HEADS * 2 * C_HEAD_DIM
    out_spec = pl.BlockSpec((1, 1, qb, LANES), lambda b, h, i: (b, i, 0, h))
    out_shape = jax.ShapeDtypeStruct((batch, half, qb, width), BF16)
    lo, hi = pl.pallas_call(
        functools.partial(_diff_kernel, lambda_init=lambda_init),
        grid=(batch, C_HEADS, half),
        in_specs=[pl.BlockSpec((4, C_HEAD_DIM), lambda b, h, i: (0, 0)),
                  pl.BlockSpec((1, qb, LANES), lambda b, h, i: (R64_CQ + h, b * nb + i, 0)),
                  pl.BlockSpec((1, qb, LANES), lambda b, h, i: (R64_CQ + h, b * nb + nb - 1 - i, 0)),
                  pl.BlockSpec((1, seq, LANES), lambda b, h, i: (R64_CK + h, b, 0)),
                  pl.BlockSpec((1, seq, LANES), lambda b, h, i: (PL_CV + h, b, 0)),
                  pl.BlockSpec((1, LANES), lambda b, h, i: (0, 0))],
        out_specs=[out_spec, out_spec],
        out_shape=[out_shape, out_shape],
        scratch_shapes=[pltpu.VMEM((half + 1, 2 * qb, chunk), F32)],
        compiler_params=_cparams(("parallel", "parallel", "arbitrary")),
        name="diff_attention",
    )(lam_params, p64, p64, p64, ppl, g_sub)
    return _pair_blocks(lo, hi, n, width)


def _merge_kernel(oa_ref, ob_ref, oc_ref, wa_ref, wb_ref, wc_ref, ga_ref, gb_ref, gc_ref, o_ref):
    ya = _dot(oa_ref[...], wa_ref[...])
    yb = _dot(ob_ref[...], wb_ref[...])
    yc = _dot(oc_ref[...], wc_ref[...])
    for c in range(ga_ref.shape[0]):
        cols = slice(c * LANES, (c + 1) * LANES)
        o_ref[:, cols] = (ga_ref[c].astype(F32) * ya[:, cols] + gb_ref[c].astype(F32) * yb[:, cols]
                          + gc_ref[c].astype(F32) * yc[:, cols]).astype(o_ref.dtype)


def merge_branches(o_a, o_b, o_c, w_pa, w_pb, w_pc, gates, *, tm=512, tn=1024):
    n = o_a.shape[0]
    d = w_pa.shape[1]
    tpb = tn // LANES
    per_branch = d // tn
    act = lambda k: pl.BlockSpec((tm, k), lambda i, j: (i, 0))
    wsp = lambda k: pl.BlockSpec((k, tn), lambda i, j: (0, j))
    gsp = lambda br: pl.BlockSpec((tpb, tm, LANES), lambda i, j: (br * per_branch + j, i, 0))
    return pl.pallas_call(
        _merge_kernel,
        grid=(n // tm, d // tn),
        in_specs=[act(o_a.shape[1]), act(o_b.shape[1]), act(o_c.shape[1]),
                  wsp(w_pa.shape[0]), wsp(w_pb.shape[0]), wsp(w_pc.shape[0]),
                  gsp(0), gsp(1), gsp(2)],
        out_specs=pl.BlockSpec((tm, tn), lambda i, j: (i, j)),
        out_shape=jax.ShapeDtypeStruct((n, d), BF16),
        compiler_params=_cparams(("parallel", "arbitrary")),
        name="merge_branches",
    )(o_a, o_b, o_c, w_pa, w_pb, w_pc, gates, gates, gates)


def _matmul_residual_kernel(a_ref, w_ref, x_ref, o_ref):
    o_ref[...] = x_ref[...] + _dot(a_ref[...], w_ref[...])


def matmul_residual(a, w, x, *, tm=512, tn=2048):
    n, k = a.shape
    d = w.shape[1]
    return pl.pallas_call(
        _matmul_residual_kernel,
        grid=(n // tm, d // tn),
        in_specs=[pl.BlockSpec((tm, k), lambda i, j: (i, 0)),
                  pl.BlockSpec((k, tn), lambda i, j: (0, j)),
                  pl.BlockSpec((tm, tn), lambda i, j: (i, j))],
        out_specs=pl.BlockSpec((tm, tn), lambda i, j: (i, j)),
        out_shape=jax.ShapeDtypeStruct((n, d), F32),
        compiler_params=_cparams(("parallel", "arbitrary")),
        name="matmul_residual",
    )(a, w, x)


def _xattn_router_kernel(x_ref, gx_ref, wq_ref, kv_ref, wo_ref, gf_ref, wr_hi_ref, wr_lo_ref, br_ref,
                         x2_ref, route_ref, counts_ref):
    x = x_ref[...]
    ms = jnp.mean(x * x, axis=-1, keepdims=True)
    h = (x * lax.rsqrt(ms + NORM_EPS) * gx_ref[...]).astype(BF16)
    q = _dot(h, wq_ref[...]).astype(BF16)
    scale = X_HEAD_DIM ** -0.5
    outs = []
    for hd in range(X_HEADS):
        s = _dot_nt(q[:, hd * LANES:(hd + 1) * LANES], kv_ref[hd]) * scale
        m = jnp.max(s, axis=-1, keepdims=True)
        e = jnp.exp(s - m)
        l = jnp.sum(e, axis=-1, keepdims=True)
        outs.append((_dot(e.astype(BF16), kv_ref[X_HEADS + hd]) / l).astype(BF16))
    x2 = x + _dot(jnp.concatenate(outs, axis=-1), wo_ref[...])
    x2_ref[...] = x2

    ms2 = jnp.mean(x2 * x2, axis=-1, keepdims=True)
    hn = x2 * lax.rsqrt(ms2 + NORM_EPS) * gf_ref[...]
    hn_hi = hn.astype(BF16)
    hn_lo = (hn - hn_hi.astype(F32)).astype(BF16)
    logits = (_dot(hn_hi, wr_hi_ref[...]) + _dot(hn_hi, wr_lo_ref[...]) + _dot(hn_lo, wr_hi_ref[...])
              + br_ref[...])

    lane = lax.broadcasted_iota(jnp.int32, logits.shape, 1).astype(F32)
    big = 1e9
    in_g = jnp.where(lane >= N_EXPERTS, jnp.where(lane < N_EXPERTS + N_GROUPS, 1.0, 0.0), 0.0) > 0.5
    lg = jnp.where(in_g, logits, NEG_INF)
    eg = jnp.where(in_g, jnp.exp(lg - jnp.max(lg, axis=-1, keepdims=True)), 0.0)
    pg = eg / jnp.sum(eg, axis=-1, keepdims=True)
    gp = jnp.max(pg, axis=-1, keepdims=True)
    gi = jnp.min(jnp.where(in_g, jnp.where(pg == gp, lane, big), big), axis=-1, keepdims=True) - N_EXPERTS
    first = EXPERTS_PER_GROUP * gi
    in_e = jnp.where(lane >= first, jnp.where(lane < first + EXPERTS_PER_GROUP, 1.0, 0.0), 0.0) > 0.5
    le = jnp.where(in_e, logits, NEG_INF)
    ee = jnp.where(in_e, jnp.exp(le - jnp.max(le, axis=-1, keepdims=True)), 0.0)
    pe = jnp.where(in_e, ee / jnp.sum(ee, axis=-1, keepdims=True), -1.0)
    p1 = jnp.max(pe, axis=-1, keepdims=True)
    i1 = jnp.min(jnp.where(pe == p1, lane, big), axis=-1, keepdims=True)
    pe2 = jnp.where(lane == i1, -1.0, pe)
    p2 = jnp.max(pe2, axis=-1, keepdims=True)
    i2 = jnp.min(jnp.where(pe2 == p2, jnp.where(in_e, jnp.where(lane == i1, big, lane), big), big),
                 axis=-1, keepdims=True)
    tot = p1 + p2
    route_ref[...] = jnp.where(lane == 0.0, i1, jnp.where(lane == 1.0, i2, jnp.where(
        lane == 2.0, gp * p1 / tot, jnp.where(lane == 3.0, gp * p2 / tot, 0.0))))

    first_step = (pl.program_id(0) == 0) & (pl.program_id(1) == 0)

    @pl.when(first_step)
    def _():
        counts_ref[...] = jnp.zeros(counts_ref.shape, F32)

    hits = jnp.where(lane == i1, 1.0, 0.0) + jnp.where(lane == i2, 1.0, 0.0)
    counts_ref[...] = counts_ref[...] + jnp.sum(hits, axis=0, keepdims=True)


def xattn_router(x, g_x, w_xq, kv, w_xo, g_ffn, wr_hi, wr_lo, b_r, batch, seq, n_mem, *, tm=512):
    n, d = x.shape
    per_b = seq // tm
    full = lambda shape: pl.BlockSpec(shape, lambda b, i: (0,) * len(shape))
    row = lambda width: pl.BlockSpec((tm, width), lambda b, i: (b * per_b + i, 0))
    return pl.pallas_call(
        _xattn_router_kernel,
        grid=(batch, per_b),
        in_specs=[row(d), full((1, d)), full((d, X_WIDTH)),
                  pl.BlockSpec((2 * X_HEADS, n_mem, LANES), lambda b, i: (0, b, 0)),
                  full((X_WIDTH, d)), full((1, d)), full((d, LANES)), full((d, LANES)), full((1, LANES))],
        out_specs=[row(d), row(LANES), full((1, LANES))],
        out_shape=[jax.ShapeDtypeStruct((n, d), F32), jax.ShapeDtypeStruct((n, LANES), F32),
                   jax.ShapeDtypeStruct((1, LANES), F32)],
        compiler_params=_cparams(("arbitrary", "arbitrary")),
        name="xattn_router",
    )(x, g_x, w_xq, kv, w_xo, g_ffn, wr_hi, wr_lo, b_r)


def _plan_kernel(route_ref, off_ref, dest_ref, run_scr):
    @pl.when(pl.program_id(0) == 0)
    def _():
        run_scr[...] = jnp.zeros(run_scr.shape, F32)

    tm = route_ref.shape[0]
    r = route_ref[...]
    lane = lax.broadcasted_iota(jnp.int32, r.shape, 1).astype(F32)
    a1 = jnp.where(lane == r[:, 0:1], 1.0, 0.0)
    a2 = jnp.where(lane == r[:, 1:2], 1.0, 0.0)
    hits = a1 + a2
    earlier = (lax.broadcasted_iota(jnp.int32, (tm, tm), 0) > lax.broadcasted_iota(jnp.int32, (tm, tm), 1))
    before = _dot(jnp.where(earlier, 1.0, 0.0).astype(BF16), hits.astype(BF16))
    slot = off_ref[...] + run_scr[...] + before
    d1 = jnp.sum(a1 * slot, axis=-1, keepdims=True)
    d2 = jnp.sum(a2 * slot, axis=-1, keepdims=True)
    dest_ref[...] = jnp.where(lane == 0.0, d1, jnp.where(lane == 1.0, d2, 0.0)).astype(jnp.int32)
    run_scr[...] = run_scr[...] + jnp.sum(hits, axis=0, keepdims=True)


def dispatch_plan(route, offsets, *, tm=512):
    n = route.shape[0]
    return pl.pallas_call(
        _plan_kernel,
        grid=(n // tm,),
        in_specs=[pl.BlockSpec((tm, LANES), lambda i: (i, 0)), pl.BlockSpec((1, LANES), lambda i: (0, 0))],
        out_specs=pl.BlockSpec((tm, LANES), lambda i: (i, 0)),
        out_shape=jax.ShapeDtypeStruct((n, LANES), jnp.int32),
        scratch_shapes=[pltpu.VMEM((1, LANES), F32)],
        compiler_params=_cparams(("arbitrary",)),
        name="dispatch_plan",
    )(route, offsets)


def _row_copy(src, src_row, dst, dst_row, sem):
    return pltpu.make_async_copy(src.at[pl.ds(src_row, 1), :], dst.at[pl.ds(dst_row, 1), :], sem)


def _dispatch_kernel(zrow_ref, zflag_ref, d1_ref, d2_ref, x_ref, xs_ref, zero_scr, sems):
    tm = x_ref.shape[0]
    tile = zero_scr.shape[0]

    @pl.when(pl.program_id(0) == 0)
    def _():
        zero_scr[...] = jnp.zeros(zero_scr.shape, F32)
        fills = [pltpu.make_async_copy(zero_scr, xs_ref.at[pl.ds(pl.multiple_of(zrow_ref[e], 8), tile), :],
                                       sems.at[2]) for e in range(zrow_ref.shape[0])]
        for e, fill in enumerate(fills):
            pl.when(zflag_ref[e] > 0)(fill.start)
        for e, fill in enumerate(fills):
            pl.when(zflag_ref[e] > 0)(fill.wait)

    def issue(r, carry):
        _row_copy(x_ref, r, xs_ref, d1_ref[r], sems.at[0]).start()
        _row_copy(x_ref, r, xs_ref, d2_ref[r], sems.at[1]).start(priority=1)
        return carry

    lax.fori_loop(0, tm, issue, 0, unroll=8)
    pltpu.make_async_copy(x_ref, xs_ref.at[pl.ds(0, tm), :], sems.at[0]).wait()
    pltpu.make_async_copy(x_ref, xs_ref.at[pl.ds(0, tm), :], sems.at[1]).wait()


def dispatch_rows(x, d1, d2, zrows, zflags, rows_total, tile, *, tm=512):
    n, d = x.shape
    return pl.pallas_call(
        _dispatch_kernel,
        grid_spec=pltpu.PrefetchScalarGridSpec(
            num_scalar_prefetch=2,
            grid=(n // tm,),
            in_specs=[pl.BlockSpec((tm,), lambda i, z, f: (i,), memory_space=pltpu.SMEM),
                      pl.BlockSpec((tm,), lambda i, z, f: (i,), memory_space=pltpu.SMEM),
                      pl.BlockSpec((tm, d), lambda i, z, f: (i, 0))],
            out_specs=pl.BlockSpec(memory_space=pl.ANY),
            scratch_shapes=[pltpu.VMEM((tile, d), F32), pltpu.SemaphoreType.DMA((3,))]),
        out_shape=jax.ShapeDtypeStruct((rows_total, d), F32),
        compiler_params=_cparams(("arbitrary",)),
        name="dispatch_rows",
    )(zrows, zflags, d1, d2, x)


def _experts_kernel(te_ref, na_ref, xs_ref, g_ref, wg_ref, wu_ref, wd_ref, y_ref):
    active = pl.program_id(0) < na_ref[0]

    @pl.when(active)
    def _():
        x = xs_ref[...]
        ms = jnp.mean(x * x, axis=-1, keepdims=True)
        hn = (x * lax.rsqrt(ms + NORM_EPS) * g_ref[...]).astype(BF16)
        gate = _dot(hn, wg_ref[0])
        up = _dot(hn, wu_ref[0])
        hid = (gate * _sigmoid(gate) * up).astype(BF16)
        y_ref[...] = _dot(hid, wd_ref[0])

    @pl.when(jnp.logical_not(active))
    def _():
        y_ref[...] = jnp.zeros(y_ref.shape, F32)


def expert_ffn(xs, g_ffn, w_gate, w_up, w_down, tile_expert, n_active, tile):
    rows_total, d = xs.shape
    _, _, dff = w_gate.shape
    last = lambda i, te, na: jnp.minimum(i, na[0] - 1)
    return pl.pallas_call(
        _experts_kernel,
        grid_spec=pltpu.PrefetchScalarGridSpec(
            num_scalar_prefetch=2,
            grid=(rows_total // tile,),
            in_specs=[pl.BlockSpec((tile, d), lambda i, te, na: (last(i, te, na), 0)),
                      pl.BlockSpec((1, d), lambda i, te, na: (0, 0)),
                      pl.BlockSpec((1, d, dff), lambda i, te, na: (te[i], 0, 0)),
                      pl.BlockSpec((1, d, dff), lambda i, te, na: (te[i], 0, 0)),
                      pl.BlockSpec((1, dff, d), lambda i, te, na: (te[i], 0, 0))],
            out_specs=pl.BlockSpec((tile, d), lambda i, te, na: (i, 0))),
        out_shape=jax.ShapeDtypeStruct((rows_total, d), F32),
        compiler_params=_cparams(("arbitrary",)),
        name="expert_ffn",
    )(tile_expert, n_active, xs, g_ffn, w_gate, w_up, w_down)


def _combine_kernel(d1_ref, d2_ref, x_ref, route_ref, y_ref, o_ref, buf, sems):
    tm = x_ref.shape[0]

    def issue(r, carry):
        _row_copy(y_ref, d1_ref[r], buf.at[0], r, sems.at[0]).start()
        _row_copy(y_ref, d2_ref[r], buf.at[1], r, sems.at[1]).start(priority=1)
        return carry

    lax.fori_loop(0, tm, issue, 0, unroll=8)
    pltpu.make_async_copy(y_ref.at[pl.ds(0, tm), :], buf.at[0], sems.at[0]).wait()
    pltpu.make_async_copy(y_ref.at[pl.ds(0, tm), :], buf.at[1], sems.at[1]).wait()
    route = route_ref[...]
    o_ref[...] = x_ref[...] + (route[:, 2:3] * buf[0] + route[:, 3:4] * buf[1])


def combine_rows(x, route, y, d1, d2, *, tm=256):
    n, d = x.shape
    return pl.pallas_call(
        _combine_kernel,
        grid=(n // tm,),
        in_specs=[pl.BlockSpec((tm,), lambda i: (i,), memory_space=pltpu.SMEM),
                  pl.BlockSpec((tm,), lambda i: (i,), memory_space=pltpu.SMEM),
                  pl.BlockSpec((tm, d), lambda i: (i, 0)),
                  pl.BlockSpec((tm, LANES), lambda i: (i, 0)),
                  pl.BlockSpec(memory_space=pl.ANY)],
        out_specs=pl.BlockSpec((tm, d), lambda i: (i, 0)),
        out_shape=jax.ShapeDtypeStruct((n, d), F32),
        scratch_shapes=[pltpu.VMEM((2, tm, d), F32), pltpu.SemaphoreType.DMA((2,))],
        compiler_params=_cparams(("arbitrary",)),
        name="combine_rows",
    )(d1, d2, x, route, y)


def routed_experts(x, route, counts, g_ffn, w_gate, w_up, w_down, *, tile=MOE_TILE):
    n, d = x.shape
    rows_total = 2 * n + N_EXPERTS * tile
    n_tiles = rows_total // tile
    cnt = counts[0, :N_EXPERTS].astype(jnp.int32)
    tiles = (cnt + tile - 1) // tile
    ends = jnp.cumsum(tiles)
    starts = ends - tiles
    n_active = ends[-1:]
    tile_ids = jnp.minimum(jnp.arange(n_tiles, dtype=jnp.int32), n_active[0] - 1)
    tile_expert = jnp.sum((tile_ids[:, None] >= ends[None, :]).astype(jnp.int32), axis=1)
    offsets = jnp.pad((starts * tile).astype(F32)[None, :], ((0, 0), (0, LANES - N_EXPERTS)))
    tail = n_active[0] + jnp.arange(N_EXPERTS, dtype=jnp.int32)
    zrows = (jnp.concatenate([jnp.maximum(ends - 1, starts), jnp.minimum(tail, n_tiles - 1)]) * tile).astype(jnp.int32)
    zflags = jnp.concatenate([tiles > 0, tail < n_tiles]).astype(jnp.int32)

    dest = dispatch_plan(route, offsets)
    d1, d2 = dest[:, 0], dest[:, 1]
    xs = dispatch_rows(x, d1, d2, zrows, zflags, rows_total, tile)
    y = expert_ffn(xs, g_ffn, w_gate, w_up, w_down, tile_expert.astype(jnp.int32), n_active.astype(jnp.int32), tile)
    return combine_rows(x, route, y, d1, d2)


def _rms_kernel(x_ref, g_ref, o_ref):
    x = x_ref[...]
    ms = jnp.mean(x * x, axis=-1, keepdims=True)
    o_ref[...] = x * lax.rsqrt(ms + NORM_EPS) * g_ref[...]


def rms_norm_rows(x, g, *, tm=512):
    n, d = x.shape
    return pl.pallas_call(
        _rms_kernel,
        grid=(n // tm,),
        in_specs=[pl.BlockSpec((tm, d), lambda i: (i, 0)), pl.BlockSpec((1, d), lambda i: (0, 0))],
        out_specs=pl.BlockSpec((tm, d), lambda i: (i, 0)),
        out_shape=jax.ShapeDtypeStruct((n, d), F32),
        compiler_params=_cparams(("parallel",)),
        name="final_norm",
    )(x, g)


def _pad_cols(w, width):
    return jnp.pad(w, ((0, 0), (0, width - w.shape[1])))


def _input_weights(w):
    cols = lambda lo, hi: w[:, lo:hi]
    w64 = jnp.concatenate([cols(_A_Q, _A_K), cols(_A_K, _A_V), cols(_C_Q, _C_K), cols(_C_K, _C_V)], axis=1)
    w128 = jnp.concatenate([cols(_B_Q, _B_KC), cols(_B_KC, _B_VC), cols(_B_KS, _B_VS), cols(_B_KW, _B_VW)], axis=1)
    per_group = 3 * B_HEADS // B_KV_GROUPS
    gate_tiles = [_pad_cols(cols(_B_GATE + g * per_group, _B_GATE + (g + 1) * per_group), LANES)
                  for g in range(B_KV_GROUPS)]
    wpl = jnp.concatenate([cols(_A_V, _B_Q), cols(_B_VC, _B_KS), cols(_B_VS, _B_KW), cols(_B_VW, _B_GATE)]
                          + gate_tiles + [cols(_C_V, _MERGE)], axis=1)
    wmg = cols(_MERGE, _IN_END)
    return w64.astype(BF16), w128.astype(BF16), wpl.astype(BF16), wmg.astype(BF16)


def kernel(x, mem, positions, g_mix, w_in, sinks_a, cmp_pos_k, cmp_pos_v, phi_k1, phi_k2, phi_v1, phi_v2,
           lq1, lk1, lq2, lk2, g_diff, w_pa, w_pb, w_pc, w_out, g_x, g_mem, w_xq, w_xkv, w_xo,
           g_ffn, w_group, b_group, w_expert, b_expert, w_gate, w_up, w_down, g_final):
    batch, seq, d = x.shape
    n_mem = mem.shape[1]
    n = batch * seq
    xf = x.reshape(n, d)
    memf = mem.reshape(batch * n_mem, d)
    pos_f = positions.reshape(n, 1).astype(F32)
    tab64 = rope_tables(pos_f, A_HEAD_DIM)
    tab128 = rope_tables(pos_f, B_HEAD_DIM)
    row = lambda v: v.reshape(1, -1)

    for l in range(DEPTH):
        lambda_init = 0.8 - 0.6 * math.exp(-0.3 * l)
        w64, w128, wpl, wmg = _input_weights(w_in[l])
        g = row(g_mix[l])
        p64 = norm_proj(xf, g, w64, tm=512, tn=w64.shape[1], rope=64, tables=tab64)
        p128 = norm_proj(xf, g, w128, tm=512, tn=w128.shape[1], rope=128, tables=tab128)
        ppl = norm_proj(xf, g, wpl, tm=512, tn=wpl.shape[1])
        gates = norm_proj(xf, g, wmg, tm=1024, tn=1024, sigmoid_out=True)

        o_a = swa_sink_attention(p64, ppl, sinks_a[l], batch, seq)
        kcmp = compress_blocks(p128[R128_KC:R128_KC + B_KV_GROUPS], cmp_pos_k[l], phi_k1[l], phi_k2[l], batch, seq)
        vcmp = compress_blocks(ppl[PL_VC:PL_VC + B_KV_GROUPS], cmp_pos_v[l], phi_v1[l], phi_v2[l], batch, seq)
        o_b = nsa_attention(p128, ppl, kcmp, vcmp, batch, seq)
        lam_params = jnp.stack([lq1[l], lk1[l], lq2[l], lk2[l]])
        o_c = diff_attention(p64, ppl, lam_params, row(g_diff[l]), lambda_init, batch, seq)

        merged = merge_branches(o_a, o_b, o_c, w_pa[l].astype(BF16), w_pb[l].astype(BF16),
                                w_pc[l].astype(BF16), gates)
        xf = matmul_residual(merged, w_out[l].astype(BF16), xf)

        kv = norm_proj(memf, row(g_mem[l]), w_xkv[l].astype(BF16), tm=512, tn=2 * X_WIDTH)
        w_r = _pad_cols(jnp.concatenate([w_expert[l], w_group[l]], axis=1), LANES)
        wr_hi = w_r.astype(BF16)
        wr_lo = (w_r - wr_hi.astype(F32)).astype(BF16)
        b_r = _pad_cols(jnp.concatenate([b_expert[l], b_group[l]])[None, :], LANES)
        xf, route, counts = xattn_router(xf, row(g_x[l]), w_xq[l].astype(BF16), kv, w_xo[l].astype(BF16),
                                         row(g_ffn[l]), wr_hi, wr_lo, b_r, batch, seq, n_mem)
        xf = routed_experts(xf, route, counts, row(g_ffn[l]), w_gate[l].astype(BF16), w_up[l].astype(BF16),
                            w_down[l].astype(BF16))

    return rms_norm_rows(xf, row(g_final)).reshape(batch, seq, d)
```

```python
import functools
import math

import jax
import jax.numpy as jnp
from jax import lax
from jax.experimental import pallas as pl
from jax.experimental.pallas import tpu as pltpu

D_MODEL = 2048
DEPTH = 2
ROPE_THETA = 10000.0
NORM_EPS = 1e-6
BLOCK_Q = 128
NEG_INF = -1e30
FORCE_SCORE = 1e6
NSA_SLC_CHUNK = 256
DIFF_Q_BLOCK = 256
MOE_TILE = 256

A_HEADS, A_KV_HEADS, A_HEAD_DIM, A_WINDOW = 8, 2, 64, 128
B_HEADS, B_KV_GROUPS, B_HEAD_DIM = 8, 2, 128
CMP_LEN, CMP_STRIDE, SLC_LEN, SLC_TOPN, B_WINDOW = 32, 16, 64, 8, 512
C_HEADS, C_HEAD_DIM = 4, 64
X_HEADS, X_HEAD_DIM = 4, 128
X_WIDTH = X_HEADS * X_HEAD_DIM
N_GROUPS, EXPERTS_PER_GROUP = 4, 4
N_EXPERTS = N_GROUPS * EXPERTS_PER_GROUP
D_FF_EXPERT = 1024

LANES = 128
VMEM_LIMIT = 52 * 1024 * 1024

_A_Q, _A_K, _A_V = 0, 512, 640
_B_Q, _B_KC, _B_VC, _B_KS, _B_VS, _B_KW, _B_VW, _B_GATE = 768, 1792, 2048, 2304, 2560, 2816, 3072, 3328
_C_Q, _C_K, _C_V, _MERGE, _IN_END = 3352, 3864, 4376, 4888, 11032

R64_AQ, R64_AK, R64_CQ, R64_CK, R64_TILES = 0, 4, 5, 9, 13
R128_BQ, R128_KC, R128_KS, R128_KW, R128_TILES = 0, 8, 10, 12, 14
PL_AV, PL_VC, PL_VS, PL_VW, PL_GATE, PL_CV, PL_TILES = 0, 1, 3, 5, 7, 9, 13

BF16 = jnp.bfloat16
F32 = jnp.float32


def _cparams(sem):
    return pltpu.CompilerParams(dimension_semantics=sem, vmem_limit_bytes=VMEM_LIMIT)


def _dot(a, b):
    return jnp.dot(a, b, preferred_element_type=F32)


def _dot_halves(a, b):
    half = a.shape[0] // 2
    return jnp.concatenate([_dot(a[0:half], b), _dot(a[half:], b)], axis=0)


def _dot_nt(a, b):
    return lax.dot_general(a, b, (((1,), (1,)), ((), ())), preferred_element_type=F32)


def _sigmoid(x):
    return 1.0 / (1.0 + jnp.exp(-x))


def _rope_table_kernel(pos_ref, inv_ref, sign_ref, cos_ref, sin_ref):
    ang = pos_ref[...] * inv_ref[...]
    cos_ref[...] = jnp.cos(ang)
    sin_ref[...] = jnp.sin(ang) * sign_ref[...]


def rope_tables(pos_f, head_dim):
    n = pos_f.shape[0]
    half = head_dim // 2
    lane = jnp.arange(LANES)
    inv = jnp.power(ROPE_THETA, -(2.0 * (lane % half).astype(F32)) / head_dim)[None, :]
    sign = jnp.where((lane % head_dim) < half, -1.0, 1.0).astype(F32)[None, :]
    tm = 2048
    return pl.pallas_call(
        _rope_table_kernel,
        grid=(n // tm,),
        in_specs=[pl.BlockSpec((tm, 1), lambda i: (i, 0)),
                  pl.BlockSpec((1, LANES), lambda i: (0, 0)),
                  pl.BlockSpec((1, LANES), lambda i: (0, 0))],
        out_specs=[pl.BlockSpec((tm, LANES), lambda i: (i, 0)),
                   pl.BlockSpec((tm, LANES), lambda i: (i, 0))],
        out_shape=[jax.ShapeDtypeStruct((n, LANES), F32)] * 2,
        compiler_params=_cparams(("parallel",)),
        name="rope_tables",
    )(pos_f, inv, sign)


def _norm_proj_kernel(*refs, rope, sigmoid_out):
    if rope:
        x_ref, g_ref, w_ref, cos_ref, sin_ref, o_ref, h_scr = refs
    else:
        x_ref, g_ref, w_ref, o_ref, h_scr = refs

    @pl.when(pl.program_id(1) == 0)
    def _():
        x = x_ref[...]
        ms = jnp.mean(x * x, axis=-1, keepdims=True)
        h_scr[...] = (x * lax.rsqrt(ms + NORM_EPS) * g_ref[...]).astype(BF16)

    acc = _dot(h_scr[...], w_ref[...])
    if rope:
        cos = cos_ref[...]
        sin = sin_ref[...]
        if rope == 64:
            first_half = (lax.broadcasted_iota(jnp.int32, cos.shape, 1) & 63) < 32
    for c in range(o_ref.shape[0]):
        a = acc[:, c * LANES:(c + 1) * LANES]
        if rope == 128:
            a = a * cos + pltpu.roll(a, 64, 1) * sin
        elif rope == 64:
            partner = jnp.where(first_half, pltpu.roll(a, 96, 1), pltpu.roll(a, 32, 1))
            a = a * cos + partner * sin
        if sigmoid_out:
            a = _sigmoid(a)
        o_ref[c] = a.astype(o_ref.dtype)


def norm_proj(x, g, w, *, tm, tn, rope=0, tables=None, sigmoid_out=False):
    n, d = x.shape
    tiles = w.shape[1] // LANES
    tpb = tn // LANES
    in_specs = [pl.BlockSpec((tm, d), lambda i, j: (i, 0)),
                pl.BlockSpec((1, d), lambda i, j: (0, 0)),
                pl.BlockSpec((d, tn), lambda i, j: (0, j))]
    args = [x, g, w]
    if rope:
        in_specs += [pl.BlockSpec((tm, LANES), lambda i, j: (i, 0))] * 2
        args += list(tables)
    return pl.pallas_call(
        functools.partial(_norm_proj_kernel, rope=rope, sigmoid_out=sigmoid_out),
        grid=(n // tm, tiles // tpb),
        in_specs=in_specs,
        out_specs=pl.BlockSpec((tpb, tm, LANES), lambda i, j: (j, i, 0)),
        out_shape=jax.ShapeDtypeStruct((tiles, n, LANES), BF16),
        scratch_shapes=[pltpu.VMEM((tm, d), BF16)],
        compiler_params=_cparams(("parallel", "arbitrary")),
        name="norm_proj",
    )(*args)


def _swa_kernel(sink_ref, q_ref, k_ref, v_ref, o_ref):
    n = pl.program_id(1)
    span = 2 * BLOCK_Q
    start = pl.multiple_of(jnp.maximum(n - 1, 0) * BLOCK_Q, BLOCK_Q)
    kk = k_ref[0, pl.ds(start, span), :]
    vv = v_ref[0, pl.ds(start, span), :]
    qpos = n * BLOCK_Q + lax.broadcasted_iota(jnp.int32, (BLOCK_Q, span), 0)
    kpos = start + lax.broadcasted_iota(jnp.int32, (BLOCK_Q, span), 1)
    rel = qpos - kpos
    scale = A_HEAD_DIM ** -0.5
    heads_per_kv = A_HEADS // A_KV_HEADS
    for t in range(A_HEADS // 2):
        qt = q_ref[t]
        outs = []
        for hh in range(2):
            h = 2 * t + hh
            g = h // heads_per_kv
            qh = qt[:, hh * 64:(hh + 1) * 64]
            kh = kk[:, g * 64:(g + 1) * 64]
            vh = vv[:, g * 64:(g + 1) * 64]
            s = _dot_nt(qh, kh) * scale
            s = jnp.where(rel >= 0, jnp.where(rel < A_WINDOW, s, NEG_INF), NEG_INF)
            sk = sink_ref[h]
            m = jnp.maximum(jnp.max(s, axis=-1, keepdims=True), sk)
            e = jnp.exp(s - m)
            denom = jnp.sum(e, axis=-1, keepdims=True) + jnp.exp(sk - m)
            outs.append(_dot(e.astype(BF16), vh) / denom)
        o_ref[:, t * LANES:(t + 1) * LANES] = jnp.concatenate(outs, axis=-1).astype(o_ref.dtype)


def swa_sink_attention(p64, ppl, sinks, batch, seq):
    n = batch * seq
    nb = seq // BLOCK_Q
    return pl.pallas_call(
        _swa_kernel,
        grid=(batch, nb),
        in_specs=[pl.BlockSpec(memory_space=pltpu.SMEM),
                  pl.BlockSpec((A_HEADS // 2, BLOCK_Q, LANES), lambda b, i: (R64_AQ // 4, b * nb + i, 0)),
                  pl.BlockSpec((1, seq, LANES), lambda b, i: (R64_AK, b, 0)),
                  pl.BlockSpec((1, seq, LANES), lambda b, i: (PL_AV, b, 0))],
        out_specs=pl.BlockSpec((BLOCK_Q, A_HEADS * A_HEAD_DIM), lambda b, i: (b * nb + i, 0)),
        out_shape=jax.ShapeDtypeStruct((n, A_HEADS * A_HEAD_DIM), BF16),
        compiler_params=_cparams(("parallel", "arbitrary")),
        name="swa_sink",
    )(sinks, p64, p64, ppl)


def _compress_kernel(t_ref, pe_ref, w1_ref, w2_ref, o_ref):
    t = t_ref[0, 0]
    half = t.shape[1]
    lo = _dot(t, w1_ref[0:half, :])
    hi = _dot(t, w1_ref[half:2 * half, :])
    rows = t.shape[0]
    hi = pltpu.roll(hi, rows - 1, 0)
    pe = jnp.broadcast_to(pe_ref[...], (8, 2 * half))
    pc = _dot(pe, w1_ref[...])[0:1, :]
    hid = lo + hi + pc
    hid = hid * _sigmoid(hid)
    o_ref[0, 0] = _dot(hid.astype(BF16), w2_ref[...]).astype(o_ref.dtype)


def compress_blocks(tiles, pos_emb, w1, w2, batch, seq):
    g = tiles.shape[0]
    rows = seq // CMP_STRIDE
    t2 = tiles.reshape(g, batch, rows, CMP_STRIDE * LANES)
    pe = pos_emb.reshape(1, CMP_LEN * LANES).astype(BF16)
    return pl.pallas_call(
        _compress_kernel,
        grid=(g, batch),
        in_specs=[pl.BlockSpec((1, 1, rows, CMP_STRIDE * LANES), lambda i, b: (i, b, 0, 0)),
                  pl.BlockSpec((1, CMP_LEN * LANES), lambda i, b: (0, 0)),
                  pl.BlockSpec((CMP_LEN * LANES, LANES), lambda i, b: (0, 0)),
                  pl.BlockSpec((LANES, LANES), lambda i, b: (0, 0))],
        out_specs=pl.BlockSpec((1, 1, rows, LANES), lambda i, b: (i, b, 0, 0)),
        out_shape=jax.ShapeDtypeStruct((g, batch, rows, LANES), BF16),
        compiler_params=_cparams(("parallel", "parallel")),
        name="compress_blocks",
    )(t2, pe, w1.astype(BF16), w2.astype(BF16))


def _nsa_kernel(q_ref, kc_ref, vc_ref, ks_ref, vs_ref, kw_ref, vw_ref, gate_ref, ovlt_ref, o_ref,
                s_scr, mx_scr, ls_scr, acc_scr, *, seq):
    n = pl.program_id(2)
    r = B_HEADS // B_KV_GROUPS
    rq = r * BLOCK_Q
    scale = B_HEAD_DIM ** -0.5
    q4 = q_ref[...].reshape(rq, LANES)
    tpos = n * BLOCK_Q + lax.broadcasted_iota(jnp.int32, (BLOCK_Q, LANES), 0)
    lane = lax.broadcasted_iota(jnp.int32, (BLOCK_Q, LANES), 1)

    nc = seq // CMP_STRIDE - 1
    s = (_dot_nt(q4, kc_ref[0, 0]) * scale).reshape(r, BLOCK_Q, LANES)
    ok = (tpos >= lane * CMP_STRIDE + (CMP_LEN - 1)) & (lane < nc)
    s = jnp.where(ok[None], s, NEG_INF)
    m = jnp.max(s, axis=-1, keepdims=True)
    e = jnp.where(ok[None], jnp.exp(s - m), 0.0)
    l = jnp.sum(e, axis=-1, keepdims=True)
    p = e / jnp.where(l > 0.0, l, 1.0)
    o_cmp = _dot_halves(p.reshape(rq, LANES).astype(BF16), vc_ref[0, 0])

    psum = p[0] + p[1] + p[2] + p[3]
    p_hi = psum.astype(BF16)
    p_mid = (psum - p_hi.astype(F32)).astype(BF16)
    p_lo = (psum - p_hi.astype(F32) - p_mid.astype(F32)).astype(BF16)
    ns = seq // SLC_LEN
    ovl_t = ovlt_ref[...]
    imp_t = (_dot_nt(ovl_t, p_hi) + _dot_nt(ovl_t, p_mid) + _dot_nt(ovl_t, p_lo))[0:ns]
    blk = lax.broadcasted_iota(jnp.int32, (ns, BLOCK_Q), 0)
    tq = n * BLOCK_Q + lax.broadcasted_iota(jnp.int32, (ns, BLOCK_Q), 1)
    cur = tq >> 6
    forced = (blk == 0) | (blk == cur) | (blk == cur - 1)
    future = blk * SLC_LEN > tq
    key = jnp.where(future, -1.0, jnp.where(forced, FORCE_SCORE, imp_t))
    rank = jnp.zeros((ns, BLOCK_Q), F32)
    for i in range(ns):
        ki = key[i:i + 1, :]
        rank = rank + jnp.where(blk > i, jnp.where(ki >= key, 1.0, 0.0), jnp.where(ki > key, 1.0, 0.0))
    sel_t = jnp.where(rank < float(min(SLC_TOPN, ns)), 1.0, 0.0)
    sel_t = jnp.concatenate([sel_t, jnp.zeros((LANES - ns, BLOCK_Q), F32)], axis=0)
    selm = sel_t.T.astype(BF16)

    chunk = s_scr.shape[2]
    halves = chunk // LANES
    mx_scr[...] = jnp.full(mx_scr.shape, NEG_INF, F32)
    ls_scr[...] = jnp.zeros(ls_scr.shape, F32)
    acc_scr[...] = jnp.zeros(acc_scr.shape, F32)
    blk_of_key = lax.broadcasted_iota(jnp.int32, (LANES, chunk), 1) >> 6
    blk_row = lax.broadcasted_iota(jnp.int32, (LANES, chunk), 0)
    kcol = lax.broadcasted_iota(jnp.int32, (BLOCK_Q, chunk), 1)
    trow = n * BLOCK_Q + lax.broadcasted_iota(jnp.int32, (BLOCK_Q, chunk), 0)
    trips = n // (chunk // BLOCK_Q) + 1

    def slc_scores(c, carry):
        base = pl.multiple_of(c * chunk, chunk)
        kk = ks_ref[0, pl.ds(base, chunk), :]
        expand = jnp.where(blk_row == blk_of_key + c * (chunk // SLC_LEN), 1.0, 0.0).astype(BF16)
        keep = jnp.where((base + kcol) <= trow, _dot(selm, expand), 0.0) > 0.5
        sc = (_dot_nt(q4, kk) * scale).reshape(r, BLOCK_Q, chunk)
        sc = jnp.where(keep[None], sc, NEG_INF).reshape(rq, chunk)
        s_scr[c] = sc
        mx = mx_scr[...]
        for j in range(halves):
            mx = jnp.maximum(mx, sc[:, j * LANES:(j + 1) * LANES])
        mx_scr[...] = mx
        return carry

    lax.fori_loop(0, trips, slc_scores, 0)
    mx_scr[...] = jnp.broadcast_to(jnp.max(mx_scr[...], axis=-1, keepdims=True), mx_scr.shape)

    def slc_values(c, carry):
        base = pl.multiple_of(c * chunk, chunk)
        vv = vs_ref[0, pl.ds(base, chunk), :]
        sc = s_scr[c]
        mb = mx_scr[...]
        es = [jnp.exp(sc[:, j * LANES:(j + 1) * LANES] - mb) for j in range(halves)]
        ls_scr[...] = ls_scr[...] + functools.reduce(lambda a, b: a + b, es)
        acc_scr[...] = acc_scr[...] + _dot_halves(jnp.concatenate(es, axis=-1).astype(BF16), vv)
        return carry

    lax.fori_loop(0, trips, slc_values, 0)
    o_slc = acc_scr[...] / jnp.sum(ls_scr[...], axis=-1, keepdims=True)

    span = B_WINDOW + BLOCK_Q
    start = pl.multiple_of(jnp.maximum(n - B_WINDOW // BLOCK_Q, 0) * BLOCK_Q, BLOCK_Q)
    rel = (n * BLOCK_Q + lax.broadcasted_iota(jnp.int32, (BLOCK_Q, span), 0)
           - start - lax.broadcasted_iota(jnp.int32, (BLOCK_Q, span), 1))
    sw = (_dot_nt(q4, kw_ref[0, pl.ds(start, span), :]) * scale).reshape(r, BLOCK_Q, span)
    sw = jnp.where((rel >= 0)[None], jnp.where((rel < B_WINDOW)[None], sw, NEG_INF), NEG_INF)
    mw = jnp.max(sw, axis=-1, keepdims=True)
    ew = jnp.exp(sw - mw)
    lw = jnp.sum(ew, axis=-1, keepdims=True).reshape(rq, 1)
    o_win = _dot_halves(ew.reshape(rq, span).astype(BF16), vw_ref[0, pl.ds(start, span), :]) / lw

    gates = _sigmoid(gate_ref[0].astype(F32))
    for hh in range(r):
        rows = slice(hh * BLOCK_Q, (hh + 1) * BLOCK_Q)
        o = (gates[:, 3 * hh:3 * hh + 1] * o_cmp[rows]
             + gates[:, 3 * hh + 1:3 * hh + 2] * o_slc[rows]
             + gates[:, 3 * hh + 2:3 * hh + 3] * o_win[rows])
        o_ref[:, hh * LANES:(hh + 1) * LANES] = o.astype(o_ref.dtype)


def _overlap_matrix_t(seq):
    nc = seq // CMP_STRIDE - 1
    ns = seq // SLC_LEN
    j = jnp.arange(LANES)[:, None]
    c = jnp.arange(LANES)[None, :]
    c_start = c * CMP_STRIDE
    hit = (c_start < (j + 1) * SLC_LEN) & (c_start + CMP_LEN > j * SLC_LEN) & (c < nc) & (j < ns)
    return hit.astype(BF16)


def nsa_attention(p128, ppl, kcmp, vcmp, batch, seq):
    n = batch * seq
    nb = seq // BLOCK_Q
    r = B_HEADS // B_KV_GROUPS
    rows = seq // CMP_STRIDE
    tok = lambda b, g, i: b * nb + i
    return pl.pallas_call(
        functools.partial(_nsa_kernel, seq=seq),
        grid=(batch, B_KV_GROUPS, nb),
        in_specs=[pl.BlockSpec((r, BLOCK_Q, LANES), lambda b, g, i: (R128_BQ // r + g, tok(b, g, i), 0)),
                  pl.BlockSpec((1, 1, rows, LANES), lambda b, g, i: (g, b, 0, 0)),
                  pl.BlockSpec((1, 1, rows, LANES), lambda b, g, i: (g, b, 0, 0)),
                  pl.BlockSpec((1, seq, LANES), lambda b, g, i: (R128_KS + g, b, 0)),
                  pl.BlockSpec((1, seq, LANES), lambda b, g, i: (PL_VS + g, b, 0)),
                  pl.BlockSpec((1, seq, LANES), lambda b, g, i: (R128_KW + g, b, 0)),
                  pl.BlockSpec((1, seq, LANES), lambda b, g, i: (PL_VW + g, b, 0)),
                  pl.BlockSpec((1, BLOCK_Q, LANES), lambda b, g, i: (PL_GATE + g, tok(b, g, i), 0)),
                  pl.BlockSpec((LANES, LANES), lambda b, g, i: (0, 0))],
        out_specs=pl.BlockSpec((BLOCK_Q, r * LANES), lambda b, g, i: (tok(b, g, i), g)),
        out_shape=jax.ShapeDtypeStruct((n, B_HEADS * B_HEAD_DIM), BF16),
        scratch_shapes=[pltpu.VMEM((seq // NSA_SLC_CHUNK, r * BLOCK_Q, NSA_SLC_CHUNK), F32),
                        pltpu.VMEM((r * BLOCK_Q, LANES), F32), pltpu.VMEM((r * BLOCK_Q, LANES), F32),
                        pltpu.VMEM((r * BLOCK_Q, LANES), F32)],
        compiler_params=_cparams(("parallel", "parallel", "arbitrary")),
        name="nsa_attention",
    )(p128, kcmp, vcmp, p128, ppl, p128, ppl, ppl, _overlap_matrix_t(seq))


def _diff_block(q_ref, k_ref, v_ref, o_ref, s_scr, slot0, first_query, n_chunks, lam, gain):
    qb = q_ref.shape[1]
    _, rows, chunk = s_scr.shape
    groups = chunk // LANES
    q = q_ref[0] * (C_HEAD_DIM ** -0.5)
    lane = lax.broadcasted_iota(jnp.int32, (qb, LANES), 1)
    zero = jnp.zeros_like(q)
    q2 = jnp.concatenate([jnp.where(lane < C_HEAD_DIM, q, zero), jnp.where(lane >= C_HEAD_DIM, q, zero)], axis=0)

    mx = jnp.full((rows, LANES), NEG_INF, F32)
    for c in range(n_chunks):
        sc = _dot_nt(q2, k_ref[0, c * chunk:(c + 1) * chunk, :])
        if c == n_chunks - 1:
            ahead = (lax.broadcasted_iota(jnp.int32, (rows, chunk), 1)
                     - (lax.broadcasted_iota(jnp.int32, (rows, chunk), 0) & (qb - 1)))
            sc = jnp.where(ahead <= first_query - c * chunk, sc, NEG_INF)
        s_scr[slot0 + c] = sc
        for j in range(groups):
            mx = jnp.maximum(mx, sc[:, j * LANES:(j + 1) * LANES])
    mb = jnp.broadcast_to(jnp.max(mx, axis=-1, keepdims=True), (rows, LANES))

    ls = jnp.zeros((rows, LANES), F32)
    acc = jnp.zeros((rows, LANES), F32)
    for c in range(n_chunks):
        sc = s_scr[slot0 + c]
        es = [jnp.exp(sc[:, j * LANES:(j + 1) * LANES] - mb) for j in range(groups)]
        ls = ls + functools.reduce(lambda a, b: a + b, es)
        acc = acc + _dot_halves(jnp.concatenate(es, axis=-1).astype(BF16),
                                v_ref[0, c * chunk:(c + 1) * chunk, :])
    a = acc / jnp.sum(ls, axis=-1, keepdims=True)
    o = a[0:qb] - lam * a[qb:rows]
    ms = jnp.mean(o * o, axis=-1, keepdims=True)
    o_ref[0, 0] = (o * lax.rsqrt(ms + NORM_EPS) * gain).astype(o_ref.dtype)


def _diff_kernel(lam_ref, qlo_ref, qhi_ref, k_ref, v_ref, g_ref, olo_ref, ohi_ref, s_scr, *, lambda_init):
    i = pl.program_id(2)
    half = pl.num_programs(2)
    qb = qlo_ref.shape[1]
    total, _, chunk = s_scr.shape
    lp = lam_ref[...]
    lam = (jnp.exp(jnp.sum(lp[0:1] * lp[1:2], axis=-1, keepdims=True))
           - jnp.exp(jnp.sum(lp[2:3] * lp[3:4], axis=-1, keepdims=True)) + lambda_init)
    gain = g_ref[...] * (1.0 - lambda_init)
    chunks_lo = lax.shift_right_logical(i * qb + qb + chunk - 1, chunk.bit_length() - 1)
    for n_lo in range(1, (half * qb + chunk - 1) // chunk + 1):
        @pl.when(chunks_lo == n_lo)
        def _():
            _diff_block(qlo_ref, k_ref, v_ref, olo_ref, s_scr, 0, i * qb, n_lo, lam, gain)
            _diff_block(qhi_ref, k_ref, v_ref, ohi_ref, s_scr, n_lo, (2 * half - 1 - i) * qb, total - n_lo,
                        lam, gain)


def _pair_blocks(lo, hi, n, width):
    return jnp.concatenate([lo, jnp.flip(hi, axis=1)], axis=1).reshape(n, width)


def diff_attention(p64, ppl, lam_params, g_sub, lambda_init, batch, seq):
    n = batch * seq
    qb = DIFF_Q_BLOCK
    chunk = 2 * qb
    nb = seq // qb
    half = nb // 2
    width = C_HEADS * 2 * C_HEAD_DIM
    out_spec = pl.BlockSpec((1, 1, qb, LANES), lambda b, h, i: (b, i, 0, h))
    out_shape = jax.ShapeDtypeStruct((batch, half, qb, width), BF16)
    lo, hi = pl.pallas_call(
        functools.partial(_diff_kernel, lambda_init=lambda_init),
        grid=(batch, C_HEADS, half),
        in_specs=[pl.BlockSpec((4, C_HEAD_DIM), lambda b, h, i: (0, 0)),
                  pl.BlockSpec((1, qb, LANES), lambda b, h, i: (R64_CQ + h, b * nb + i, 0)),
                  pl.BlockSpec((1, qb, LANES), lambda b, h, i: (R64_CQ + h, b * nb + nb - 1 - i, 0)),
                  pl.BlockSpec((1, seq, LANES), lambda b, h, i: (R64_CK + h, b, 0)),
                  pl.BlockSpec((1, seq, LANES), lambda b, h, i: (PL_CV + h, b, 0)),
                  pl.BlockSpec((1, LANES), lambda b, h, i: (0, 0))],
        out_specs=[out_spec, out_spec],
        out_shape=[out_shape, out_shape],
        scratch_shapes=[pltpu.VMEM((half + 1, 2 * qb, chunk), F32)],
        compiler_params=_cparams(("parallel", "parallel", "arbitrary")),
        name="diff_attention",
    )(lam_params, p64, p64, p64, ppl, g_sub)
    return _pair_blocks(lo, hi, n, width)


def _merge_kernel(oa_ref, ob_ref, oc_ref, wa_ref, wb_ref, wc_ref, ga_ref, gb_ref, gc_ref, o_ref):
    ya = _dot(oa_ref[...], wa_ref[...])
    yb = _dot(ob_ref[...], wb_ref[...])
    yc = _dot(oc_ref[...], wc_ref[...])
    for c in range(ga_ref.shape[0]):
        cols = slice(c * LANES, (c + 1) * LANES)
        o_ref[:, cols] = (ga_ref[c].astype(F32) * ya[:, cols] + gb_ref[c].astype(F32) * yb[:, cols]
                          + gc_ref[c].astype(F32) * yc[:, cols]).astype(o_ref.dtype)


def merge_branches(o_a, o_b, o_c, w_pa, w_pb, w_pc, gates, *, tm=512, tn=1024):
    n = o_a.shape[0]
    d = w_pa.shape[1]
    tpb = tn // LANES
    per_branch = d // tn
    act = lambda k: pl.BlockSpec((tm, k), lambda i, j: (i, 0))
    wsp = lambda k: pl.BlockSpec((k, tn), lambda i, j: (0, j))
    gsp = lambda br: pl.BlockSpec((tpb, tm, LANES), lambda i, j: (br * per_branch + j, i, 0))
    return pl.pallas_call(
        _merge_kernel,
        grid=(n // tm, d // tn),
        in_specs=[act(o_a.shape[1]), act(o_b.shape[1]), act(o_c.shape[1]),
                  wsp(w_pa.shape[0]), wsp(w_pb.shape[0]), wsp(w_pc.shape[0]),
                  gsp(0), gsp(1), gsp(2)],
        out_specs=pl.BlockSpec((tm, tn), lambda i, j: (i, j)),
        out_shape=jax.ShapeDtypeStruct((n, d), BF16),
        compiler_params=_cparams(("parallel", "arbitrary")),
        name="merge_branches",
    )(o_a, o_b, o_c, w_pa, w_pb, w_pc, gates, gates, gates)


def _matmul_residual_kernel(a_ref, w_ref, x_ref, o_ref):
    o_ref[...] = x_ref[...] + _dot(a_ref[...], w_ref[...])


def matmul_residual(a, w, x, *, tm=512, tn=2048):
    n, k = a.shape
    d = w.shape[1]
    return pl.pallas_call(
        _matmul_residual_kernel,
        grid=(n // tm, d // tn),
        in_specs=[pl.BlockSpec((tm, k), lambda i, j: (i, 0)),
                  pl.BlockSpec((k, tn), lambda i, j: (0, j)),
                  pl.BlockSpec((tm, tn), lambda i, j: (i, j))],
        out_specs=pl.BlockSpec((tm, tn), lambda i, j: (i, j)),
        out_shape=jax.ShapeDtypeStruct((n, d), F32),
        compiler_params=_cparams(("parallel", "arbitrary")),
        name="matmul_residual",
    )(a, w, x)


def _xattn_router_kernel(x_ref, gx_ref, wq_ref, kv_ref, wo_ref, gf_ref, wr_hi_ref, wr_lo_ref, br_ref,
                         x2_ref, route_ref, counts_ref):
    x = x_ref[...]
    ms = jnp.mean(x * x, axis=-1, keepdims=True)
    h = (x * lax.rsqrt(ms + NORM_EPS) * gx_ref[...]).astype(BF16)
    q = _dot(h, wq_ref[...]).astype(BF16)
    scale = X_HEAD_DIM ** -0.5
    outs = []
    for hd in range(X_HEADS):
        s = _dot_nt(q[:, hd * LANES:(hd + 1) * LANES], kv_ref[hd]) * scale
        m = jnp.max(s, axis=-1, keepdims=True)
        e = jnp.exp(s - m)
        l = jnp.sum(e, axis=-1, keepdims=True)
        outs.append((_dot_halves(e.astype(BF16), kv_ref[X_HEADS + hd]) / l).astype(BF16))
    x2 = x + _dot(jnp.concatenate(outs, axis=-1), wo_ref[...])
    x2_ref[...] = x2

    ms2 = jnp.mean(x2 * x2, axis=-1, keepdims=True)
    hn = x2 * lax.rsqrt(ms2 + NORM_EPS) * gf_ref[...]
    hn_hi = hn.astype(BF16)
    hn_lo = (hn - hn_hi.astype(F32)).astype(BF16)
    logits = (_dot_halves(hn_hi, wr_hi_ref[...]) + _dot_halves(hn_hi, wr_lo_ref[...])
              + _dot_halves(hn_lo, wr_hi_ref[...])
              + br_ref[...])

    lane = lax.broadcasted_iota(jnp.int32, logits.shape, 1).astype(F32)
    big = 1e9
    in_g = jnp.where(lane >= N_EXPERTS, jnp.where(lane < N_EXPERTS + N_GROUPS, 1.0, 0.0), 0.0) > 0.5
    lg = jnp.where(in_g, logits, NEG_INF)
    eg = jnp.where(in_g, jnp.exp(lg - jnp.max(lg, axis=-1, keepdims=True)), 0.0)
    pg = eg / jnp.sum(eg, axis=-1, keepdims=True)
    gp = jnp.max(pg, axis=-1, keepdims=True)
    gi = jnp.min(jnp.where(in_g, jnp.where(pg == gp, lane, big), big), axis=-1, keepdims=True) - N_EXPERTS
    first = EXPERTS_PER_GROUP * gi
    in_e = jnp.where(lane >= first, jnp.where(lane < first + EXPERTS_PER_GROUP, 1.0, 0.0), 0.0) > 0.5
    le = jnp.where(in_e, logits, NEG_INF)
    ee = jnp.where(in_e, jnp.exp(le - jnp.max(le, axis=-1, keepdims=True)), 0.0)
    pe = jnp.where(in_e, ee / jnp.sum(ee, axis=-1, keepdims=True), -1.0)
    p1 = jnp.max(pe, axis=-1, keepdims=True)
    i1 = jnp.min(jnp.where(pe == p1, lane, big), axis=-1, keepdims=True)
    pe2 = jnp.where(lane == i1, -1.0, pe)
    p2 = jnp.max(pe2, axis=-1, keepdims=True)
    i2 = jnp.min(jnp.where(pe2 == p2, jnp.where(in_e, jnp.where(lane == i1, big, lane), big), big),
                 axis=-1, keepdims=True)
    tot = p1 + p2
    route_ref[...] = jnp.where(lane == 0.0, i1, jnp.where(lane == 1.0, i2, jnp.where(
        lane == 2.0, gp * p1 / tot, jnp.where(lane == 3.0, gp * p2 / tot, 0.0))))

    first_step = (pl.program_id(0) == 0) & (pl.program_id(1) == 0)

    @pl.when(first_step)
    def _():
        counts_ref[...] = jnp.zeros(counts_ref.shape, F32)

    hits = jnp.where(lane == i1, 1.0, 0.0) + jnp.where(lane == i2, 1.0, 0.0)
    counts_ref[...] = counts_ref[...] + jnp.sum(hits, axis=0, keepdims=True)


def xattn_router(x, g_x, w_xq, kv, w_xo, g_ffn, wr_hi, wr_lo, b_r, batch, seq, n_mem, *, tm=512):
    n, d = x.shape
    per_b = seq // tm
    full = lambda shape: pl.BlockSpec(shape, lambda b, i: (0,) * len(shape))
    row = lambda width: pl.BlockSpec((tm, width), lambda b, i: (b * per_b + i, 0))
    return pl.pallas_call(
        _xattn_router_kernel,
        grid=(batch, per_b),
        in_specs=[row(d), full((1, d)), full((d, X_WIDTH)),
                  pl.BlockSpec((2 * X_HEADS, n_mem, LANES), lambda b, i: (0, b, 0)),
                  full((X_WIDTH, d)), full((1, d)), full((d, LANES)), full((d, LANES)), full((1, LANES))],
        out_specs=[row(d), row(LANES), full((1, LANES))],
        out_shape=[jax.ShapeDtypeStruct((n, d), F32), jax.ShapeDtypeStruct((n, LANES), F32),
                   jax.ShapeDtypeStruct((1, LANES), F32)],
        compiler_params=_cparams(("arbitrary", "arbitrary")),
        name="xattn_router",
    )(x, g_x, w_xq, kv, w_xo, g_ffn, wr_hi, wr_lo, b_r)


def _plan_kernel(route_ref, off_ref, dest_ref, run_scr):
    @pl.when(pl.program_id(0) == 0)
    def _():
        run_scr[...] = jnp.zeros(run_scr.shape, F32)

    tm = route_ref.shape[0]
    r = route_ref[...]
    lane = lax.broadcasted_iota(jnp.int32, r.shape, 1).astype(F32)
    a1 = jnp.where(lane == r[:, 0:1], 1.0, 0.0)
    a2 = jnp.where(lane == r[:, 1:2], 1.0, 0.0)
    hits = a1 + a2
    earlier = (lax.broadcasted_iota(jnp.int32, (tm, tm), 0) > lax.broadcasted_iota(jnp.int32, (tm, tm), 1))
    before = _dot_halves(jnp.where(earlier, 1.0, 0.0).astype(BF16), hits.astype(BF16))
    slot = off_ref[...] + run_scr[...] + before
    d1 = jnp.sum(a1 * slot, axis=-1, keepdims=True)
    d2 = jnp.sum(a2 * slot, axis=-1, keepdims=True)
    dest_ref[...] = jnp.where(lane == 0.0, d1, jnp.where(lane == 1.0, d2, 0.0)).astype(jnp.int32)
    run_scr[...] = run_scr[...] + jnp.sum(hits, axis=0, keepdims=True)


def dispatch_plan(route, offsets, *, tm=512):
    n = route.shape[0]
    return pl.pallas_call(
        _plan_kernel,
        grid=(n // tm,),
        in_specs=[pl.BlockSpec((tm, LANES), lambda i: (i, 0)), pl.BlockSpec((1, LANES), lambda i: (0, 0))],
        out_specs=pl.BlockSpec((tm, LANES), lambda i: (i, 0)),
        out_shape=jax.ShapeDtypeStruct((n, LANES), jnp.int32),
        scratch_shapes=[pltpu.VMEM((1, LANES), F32)],
        compiler_params=_cparams(("arbitrary",)),
        name="dispatch_plan",
    )(route, offsets)


def _row_copy(src, src_row, dst, dst_row, sem):
    return pltpu.make_async_copy(src.at[pl.ds(src_row, 1), :], dst.at[pl.ds(dst_row, 1), :], sem)


def _dispatch_kernel(zrow_ref, zflag_ref, d1_ref, d2_ref, x_ref, xs_ref, zero_scr, sems):
    tm = x_ref.shape[0]
    tile = zero_scr.shape[0]

    @pl.when(pl.program_id(0) == 0)
    def _():
        zero_scr[...] = jnp.zeros(zero_scr.shape, F32)
        fills = [pltpu.make_async_copy(zero_scr, xs_ref.at[pl.ds(pl.multiple_of(zrow_ref[e], 8), tile), :],
                                       sems.at[2]) for e in range(zrow_ref.shape[0])]
        for e, fill in enumerate(fills):
            pl.when(zflag_ref[e] > 0)(fill.start)
        for e, fill in enumerate(fills):
            pl.when(zflag_ref[e] > 0)(fill.wait)

    def issue(r, carry):
        _row_copy(x_ref, r, xs_ref, d1_ref[r], sems.at[0]).start()
        _row_copy(x_ref, r, xs_ref, d2_ref[r], sems.at[1]).start(priority=1)
        return carry

    lax.fori_loop(0, tm, issue, 0, unroll=8)
    pltpu.make_async_copy(x_ref, xs_ref.at[pl.ds(0, tm), :], sems.at[0]).wait()
    pltpu.make_async_copy(x_ref, xs_ref.at[pl.ds(0, tm), :], sems.at[1]).wait()


def dispatch_rows(x, d1, d2, zrows, zflags, rows_total, tile, *, tm=512):
    n, d = x.shape
    return pl.pallas_call(
        _dispatch_kernel,
        grid_spec=pltpu.PrefetchScalarGridSpec(
            num_scalar_prefetch=2,
            grid=(n // tm,),
            in_specs=[pl.BlockSpec((tm,), lambda i, z, f: (i,), memory_space=pltpu.SMEM),
                      pl.BlockSpec((tm,), lambda i, z, f: (i,), memory_space=pltpu.SMEM),
                      pl.BlockSpec((tm, d), lambda i, z, f: (i, 0))],
            out_specs=pl.BlockSpec(memory_space=pl.ANY),
            scratch_shapes=[pltpu.VMEM((tile, d), F32), pltpu.SemaphoreType.DMA((3,))]),
        out_shape=jax.ShapeDtypeStruct((rows_total, d), F32),
        compiler_params=_cparams(("arbitrary",)),
        name="dispatch_rows",
    )(zrows, zflags, d1, d2, x)


def _experts_kernel(te_ref, na_ref, xs_ref, g_ref, wg_ref, wu_ref, wd_ref, y_ref):
    active = pl.program_id(0) < na_ref[0]

    @pl.when(active)
    def _():
        x = xs_ref[...]
        ms = jnp.mean(x * x, axis=-1, keepdims=True)
        hn = (x * lax.rsqrt(ms + NORM_EPS) * g_ref[...]).astype(BF16)
        gate = _dot(hn, wg_ref[0])
        up = _dot(hn, wu_ref[0])
        hid = (gate * _sigmoid(gate) * up).astype(BF16)
        y_ref[...] = _dot(hid, wd_ref[0])

    @pl.when(jnp.logical_not(active))
    def _():
        y_ref[...] = jnp.zeros(y_ref.shape, F32)


def expert_ffn(xs, g_ffn, w_gate, w_up, w_down, tile_expert, n_active, tile):
    rows_total, d = xs.shape
    _, _, dff = w_gate.shape
    last = lambda i, te, na: jnp.minimum(i, na[0] - 1)
    return pl.pallas_call(
        _experts_kernel,
        grid_spec=pltpu.PrefetchScalarGridSpec(
            num_scalar_prefetch=2,
            grid=(rows_total // tile,),
            in_specs=[pl.BlockSpec((tile, d), lambda i, te, na: (last(i, te, na), 0)),
                      pl.BlockSpec((1, d), lambda i, te, na: (0, 0)),
                      pl.BlockSpec((1, d, dff), lambda i, te, na: (te[i], 0, 0)),
                      pl.BlockSpec((1, d, dff), lambda i, te, na: (te[i], 0, 0)),
                      pl.BlockSpec((1, dff, d), lambda i, te, na: (te[i], 0, 0))],
            out_specs=pl.BlockSpec((tile, d), lambda i, te, na: (i, 0))),
        out_shape=jax.ShapeDtypeStruct((rows_total, d), F32),
        compiler_params=_cparams(("arbitrary",)),
        name="expert_ffn",
    )(tile_expert, n_active, xs, g_ffn, w_gate, w_up, w_down)


def _combine_kernel(*refs, final_norm):
    if final_norm:
        d1_ref, d2_ref, x_ref, route_ref, y_ref, g_ref, o_ref, buf, sems = refs
    else:
        d1_ref, d2_ref, x_ref, route_ref, y_ref, o_ref, buf, sems = refs
    tm = x_ref.shape[0]

    def issue(r, carry):
        _row_copy(y_ref, d1_ref[r], buf.at[0], r, sems.at[0]).start()
        _row_copy(y_ref, d2_ref[r], buf.at[1], r, sems.at[1]).start(priority=1)
        return carry

    lax.fori_loop(0, tm, issue, 0, unroll=8)
    pltpu.make_async_copy(y_ref.at[pl.ds(0, tm), :], buf.at[0], sems.at[0]).wait()
    pltpu.make_async_copy(y_ref.at[pl.ds(0, tm), :], buf.at[1], sems.at[1]).wait()
    route = route_ref[...]
    o = x_ref[...] + (route[:, 2:3] * buf[0] + route[:, 3:4] * buf[1])
    if final_norm:
        ms = jnp.mean(o * o, axis=-1, keepdims=True)
        o = o * lax.rsqrt(ms + NORM_EPS) * g_ref[...]
    o_ref[...] = o


def combine_rows(x, route, y, d1, d2, final_gain=None, *, tm=512):
    n, d = x.shape
    in_specs = [pl.BlockSpec((tm,), lambda i: (i,), memory_space=pltpu.SMEM),
                pl.BlockSpec((tm,), lambda i: (i,), memory_space=pltpu.SMEM),
                pl.BlockSpec((tm, d), lambda i: (i, 0)),
                pl.BlockSpec((tm, LANES), lambda i: (i, 0)),
                pl.BlockSpec(memory_space=pl.ANY)]
    args = [d1, d2, x, route, y]
    if final_gain is not None:
        in_specs.append(pl.BlockSpec((1, d), lambda i: (0, 0)))
        args.append(final_gain)
    return pl.pallas_call(
        functools.partial(_combine_kernel, final_norm=final_gain is not None),
        grid=(n // tm,),
        in_specs=in_specs,
        out_specs=pl.BlockSpec((tm, d), lambda i: (i, 0)),
        out_shape=jax.ShapeDtypeStruct((n, d), F32),
        scratch_shapes=[pltpu.VMEM((2, tm, d), F32), pltpu.SemaphoreType.DMA((2,))],
        compiler_params=_cparams(("arbitrary",)),
        name="combine_rows",
    )(*args)


def routed_experts(x, route, counts, g_ffn, w_gate, w_up, w_down, final_gain=None, *, tile=MOE_TILE):
    n, d = x.shape
    rows_total = 2 * n + N_EXPERTS * tile
    n_tiles = rows_total // tile
    cnt = counts[0, :N_EXPERTS].astype(jnp.int32)
    tiles = (cnt + tile - 1) // tile
    ends = jnp.cumsum(tiles)
    starts = ends - tiles
    n_active = ends[-1:]
    tile_ids = jnp.minimum(jnp.arange(n_tiles, dtype=jnp.int32), n_active[0] - 1)
    tile_expert = jnp.sum((tile_ids[:, None] >= ends[None, :]).astype(jnp.int32), axis=1)
    offsets = jnp.pad((starts * tile).astype(F32)[None, :], ((0, 0), (0, LANES - N_EXPERTS)))
    tail = n_active[0] + jnp.arange(N_EXPERTS, dtype=jnp.int32)
    zrows = (jnp.concatenate([jnp.maximum(ends - 1, starts), jnp.minimum(tail, n_tiles - 1)]) * tile).astype(jnp.int32)
    zflags = jnp.concatenate([tiles > 0, tail < n_tiles]).astype(jnp.int32)

    dest = dispatch_plan(route, offsets)
    d1, d2 = dest[:, 0], dest[:, 1]
    xs = dispatch_rows(x, d1, d2, zrows, zflags, rows_total, tile)
    y = expert_ffn(xs, g_ffn, w_gate, w_up, w_down, tile_expert.astype(jnp.int32), n_active.astype(jnp.int32), tile)
    return combine_rows(x, route, y, d1, d2, final_gain)


def _pad_cols(w, width):
    return jnp.pad(w, ((0, 0), (0, width - w.shape[1])))


def _regroup_kernel(w_ref, o64_ref, o128_ref, opl_ref, omg_ref):
    w = w_ref[...]
    cols = lambda lo, hi: w[:, lo:hi]
    o64_ref[...] = jnp.concatenate(
        [cols(_A_Q, _A_K), cols(_A_K, _A_V), cols(_C_Q, _C_K), cols(_C_K, _C_V)], axis=1).astype(BF16)
    o128_ref[...] = jnp.concatenate(
        [cols(_B_Q, _B_KC), cols(_B_KC, _B_VC), cols(_B_KS, _B_VS), cols(_B_KW, _B_VW)], axis=1).astype(BF16)
    per_group = 3 * B_HEADS // B_KV_GROUPS
    pad = jnp.zeros((w.shape[0], LANES - per_group), w.dtype)
    gate_tiles = []
    for g in range(B_KV_GROUPS):
        gate_tiles += [cols(_B_GATE + g * per_group, _B_GATE + (g + 1) * per_group), pad]
    opl_ref[...] = jnp.concatenate(
        [cols(_A_V, _B_Q), cols(_B_VC, _B_KS), cols(_B_VS, _B_KW), cols(_B_VW, _B_GATE)]
        + gate_tiles + [cols(_C_V, _MERGE)], axis=1).astype(BF16)
    omg_ref[...] = cols(_MERGE, _IN_END).astype(BF16)


def _input_weights(w, *, tr=128):
    d = w.shape[0]
    widths = (R64_TILES * LANES, R128_TILES * LANES, PL_TILES * LANES, _IN_END - _MERGE)
    return pl.pallas_call(
        _regroup_kernel,
        grid=(d // tr,),
        in_specs=[pl.BlockSpec((tr, w.shape[1]), lambda i: (i, 0))],
        out_specs=[pl.BlockSpec((tr, wd), lambda i: (i, 0)) for wd in widths],
        out_shape=[jax.ShapeDtypeStruct((d, wd), BF16) for wd in widths],
        compiler_params=_cparams(("parallel",)),
        name="regroup_w_in",
    )(w)


def kernel(x, mem, positions, g_mix, w_in, sinks_a, cmp_pos_k, cmp_pos_v, phi_k1, phi_k2, phi_v1, phi_v2,
           lq1, lk1, lq2, lk2, g_diff, w_pa, w_pb, w_pc, w_out, g_x, g_mem, w_xq, w_xkv, w_xo,
           g_ffn, w_group, b_group, w_expert, b_expert, w_gate, w_up, w_down, g_final):
    batch, seq, d = x.shape
    n_mem = mem.shape[1]
    n = batch * seq
    xf = x.reshape(n, d)
    memf = mem.reshape(batch * n_mem, d)
    pos_f = positions.reshape(n, 1).astype(F32)
    tab64 = rope_tables(pos_f, A_HEAD_DIM)
    tab128 = rope_tables(pos_f, B_HEAD_DIM)
    row = lambda v: v.reshape(1, -1)

    for l in range(DEPTH):
        lambda_init = 0.8 - 0.6 * math.exp(-0.3 * l)
        w64, w128, wpl, wmg = _input_weights(w_in[l])
        g = row(g_mix[l])
        p64 = norm_proj(xf, g, w64, tm=512, tn=w64.shape[1], rope=64, tables=tab64)
        p128 = norm_proj(xf, g, w128, tm=512, tn=w128.shape[1], rope=128, tables=tab128)
        ppl = norm_proj(xf, g, wpl, tm=512, tn=wpl.shape[1])
        gates = norm_proj(xf, g, wmg, tm=1024, tn=1024, sigmoid_out=True)

        o_a = swa_sink_attention(p64, ppl, sinks_a[l], batch, seq)
        kcmp = compress_blocks(p128[R128_KC:R128_KC + B_KV_GROUPS], cmp_pos_k[l], phi_k1[l], phi_k2[l], batch, seq)
        vcmp = compress_blocks(ppl[PL_VC:PL_VC + B_KV_GROUPS], cmp_pos_v[l], phi_v1[l], phi_v2[l], batch, seq)
        o_b = nsa_attention(p128, ppl, kcmp, vcmp, batch, seq)
        lam_params = jnp.stack([lq1[l], lk1[l], lq2[l], lk2[l]])
        o_c = diff_attention(p64, ppl, lam_params, row(g_diff[l]), lambda_init, batch, seq)

        merged = merge_branches(o_a, o_b, o_c, w_pa[l].astype(BF16), w_pb[l].astype(BF16),
                                w_pc[l].astype(BF16), gates)
        xf = matmul_residual(merged, w_out[l].astype(BF16), xf)

        kv = norm_proj(memf, row(g_mem[l]), w_xkv[l].astype(BF16), tm=512, tn=2 * X_WIDTH)
        w_r = _pad_cols(jnp.concatenate([w_expert[l], w_group[l]], axis=1), LANES)
        wr_hi = w_r.astype(BF16)
        wr_lo = (w_r - wr_hi.astype(F32)).astype(BF16)
        b_r = _pad_cols(jnp.concatenate([b_expert[l], b_group[l]])[None, :], LANES)
        xf, route, counts = xattn_router(xf, row(g_x[l]), w_xq[l].astype(BF16), kv, w_xo[l].astype(BF16),
                                         row(g_ffn[l]), wr_hi, wr_lo, b_r, batch, seq, n_mem)
        xf = routed_experts(xf, route, counts, row(g_ffn[l]), w_gate[l].astype(BF16), w_up[l].astype(BF16),
                            w_down[l].astype(BF16), final_gain=row(g_final) if l == DEPTH - 1 else None)

    return xf.reshape(batch, seq, d)
```

```python
import functools
import math

import jax
import jax.numpy as jnp
from jax import lax
from jax.experimental import pallas as pl
from jax.experimental.pallas import tpu as pltpu

D_MODEL = 2048
DEPTH = 2
ROPE_THETA = 10000.0
NORM_EPS = 1e-6
BLOCK_Q = 128
NEG_INF = -1e30
FORCE_SCORE = 1e6
NSA_Q_BLOCK = 256
NSA_SLC_CHUNK = 256
DIFF_Q_BLOCK = 256
MOE_TILE = 256

A_HEADS, A_KV_HEADS, A_HEAD_DIM, A_WINDOW = 8, 2, 64, 128
B_HEADS, B_KV_GROUPS, B_HEAD_DIM = 8, 2, 128
CMP_LEN, CMP_STRIDE, SLC_LEN, SLC_TOPN, B_WINDOW = 32, 16, 64, 8, 512
C_HEADS, C_HEAD_DIM = 4, 64
X_HEADS, X_HEAD_DIM = 4, 128
X_WIDTH = X_HEADS * X_HEAD_DIM
N_GROUPS, EXPERTS_PER_GROUP = 4, 4
N_EXPERTS = N_GROUPS * EXPERTS_PER_GROUP
D_FF_EXPERT = 1024

LANES = 128
VMEM_LIMIT = 52 * 1024 * 1024

_A_Q, _A_K, _A_V = 0, 512, 640
_B_Q, _B_KC, _B_VC, _B_KS, _B_VS, _B_KW, _B_VW, _B_GATE = 768, 1792, 2048, 2304, 2560, 2816, 3072, 3328
_C_Q, _C_K, _C_V, _MERGE, _IN_END = 3352, 3864, 4376, 4888, 11032

R64_AQ, R64_AK, R64_CQ, R64_CK, R64_TILES = 0, 4, 5, 9, 13
R128_BQ, R128_KC, R128_KS, R128_KW, R128_TILES = 0, 8, 10, 12, 14
PL_AV, PL_VC, PL_VS, PL_VW, PL_GATE, PL_CV, PL_TILES = 0, 1, 3, 5, 7, 9, 13

BF16 = jnp.bfloat16
F32 = jnp.float32


def _cparams(sem):
    return pltpu.CompilerParams(dimension_semantics=sem, vmem_limit_bytes=VMEM_LIMIT)


def _dot(a, b):
    return jnp.dot(a, b, preferred_element_type=F32)


def _dot_halves(a, b):
    half = a.shape[0] // 2
    return jnp.concatenate([_dot(a[0:half], b), _dot(a[half:], b)], axis=0)


def _dot_nt(a, b):
    return lax.dot_general(a, b, (((1,), (1,)), ((), ())), preferred_element_type=F32)


def _sigmoid(x):
    return 1.0 / (1.0 + jnp.exp(-x))


def _rope_table_kernel(pos_ref, inv_ref, sign_ref, cos_ref, sin_ref):
    ang = pos_ref[...] * inv_ref[...]
    cos_ref[...] = jnp.cos(ang)
    sin_ref[...] = jnp.sin(ang) * sign_ref[...]


def rope_tables(pos_f, head_dim):
    n = pos_f.shape[0]
    half = head_dim // 2
    lane = jnp.arange(LANES)
    inv = jnp.power(ROPE_THETA, -(2.0 * (lane % half).astype(F32)) / head_dim)[None, :]
    sign = jnp.where((lane % head_dim) < half, -1.0, 1.0).astype(F32)[None, :]
    tm = 2048
    return pl.pallas_call(
        _rope_table_kernel,
        grid=(n // tm,),
        in_specs=[pl.BlockSpec((tm, 1), lambda i: (i, 0)),
                  pl.BlockSpec((1, LANES), lambda i: (0, 0)),
                  pl.BlockSpec((1, LANES), lambda i: (0, 0))],
        out_specs=[pl.BlockSpec((tm, LANES), lambda i: (i, 0)),
                   pl.BlockSpec((tm, LANES), lambda i: (i, 0))],
        out_shape=[jax.ShapeDtypeStruct((n, LANES), F32)] * 2,
        compiler_params=_cparams(("parallel",)),
        name="rope_tables",
    )(pos_f, inv, sign)


def _norm_proj_kernel(*refs, rope, sigmoid_out):
    if rope:
        x_ref, g_ref, w_ref, cos_ref, sin_ref, o_ref, h_scr = refs
    else:
        x_ref, g_ref, w_ref, o_ref, h_scr = refs

    @pl.when(pl.program_id(1) == 0)
    def _():
        x = x_ref[...]
        ms = jnp.mean(x * x, axis=-1, keepdims=True)
        h_scr[...] = (x * lax.rsqrt(ms + NORM_EPS) * g_ref[...]).astype(BF16)

    acc = _dot(h_scr[...], w_ref[...])
    if rope:
        cos = cos_ref[...]
        sin = sin_ref[...]
        if rope == 64:
            first_half = (lax.broadcasted_iota(jnp.int32, cos.shape, 1) & 63) < 32
    for c in range(o_ref.shape[0]):
        a = acc[:, c * LANES:(c + 1) * LANES]
        if rope == 128:
            a = a * cos + pltpu.roll(a, 64, 1) * sin
        elif rope == 64:
            partner = jnp.where(first_half, pltpu.roll(a, 96, 1), pltpu.roll(a, 32, 1))
            a = a * cos + partner * sin
        if sigmoid_out:
            a = _sigmoid(a)
        o_ref[c] = a.astype(o_ref.dtype)


def norm_proj(x, g, w, *, tm, tn, rope=0, tables=None, sigmoid_out=False):
    n, d = x.shape
    tiles = w.shape[1] // LANES
    tpb = tn // LANES
    in_specs = [pl.BlockSpec((tm, d), lambda i, j: (i, 0)),
                pl.BlockSpec((1, d), lambda i, j: (0, 0)),
                pl.BlockSpec((d, tn), lambda i, j: (0, j))]
    args = [x, g, w]
    if rope:
        in_specs += [pl.BlockSpec((tm, LANES), lambda i, j: (i, 0))] * 2
        args += list(tables)
    return pl.pallas_call(
        functools.partial(_norm_proj_kernel, rope=rope, sigmoid_out=sigmoid_out),
        grid=(n // tm, tiles // tpb),
        in_specs=in_specs,
        out_specs=pl.BlockSpec((tpb, tm, LANES), lambda i, j: (j, i, 0)),
        out_shape=jax.ShapeDtypeStruct((tiles, n, LANES), BF16),
        scratch_shapes=[pltpu.VMEM((tm, d), BF16)],
        compiler_params=_cparams(("parallel", "arbitrary")),
        name="norm_proj",
    )(*args)


def _swa_kernel(sink_ref, q_ref, k_ref, v_ref, o_ref):
    n = pl.program_id(1)
    span = 2 * BLOCK_Q
    start = pl.multiple_of(jnp.maximum(n - 1, 0) * BLOCK_Q, BLOCK_Q)
    kk = k_ref[0, pl.ds(start, span), :]
    vv = v_ref[0, pl.ds(start, span), :]
    qpos = n * BLOCK_Q + lax.broadcasted_iota(jnp.int32, (BLOCK_Q, span), 0)
    kpos = start + lax.broadcasted_iota(jnp.int32, (BLOCK_Q, span), 1)
    rel = qpos - kpos
    scale = A_HEAD_DIM ** -0.5
    heads_per_kv = A_HEADS // A_KV_HEADS
    for t in range(A_HEADS // 2):
        qt = q_ref[t]
        outs = []
        for hh in range(2):
            h = 2 * t + hh
            g = h // heads_per_kv
            qh = qt[:, hh * 64:(hh + 1) * 64]
            kh = kk[:, g * 64:(g + 1) * 64]
            vh = vv[:, g * 64:(g + 1) * 64]
            s = _dot_nt(qh, kh) * scale
            s = jnp.where(rel >= 0, jnp.where(rel < A_WINDOW, s, NEG_INF), NEG_INF)
            sk = sink_ref[h]
            m = jnp.maximum(jnp.max(s, axis=-1, keepdims=True), sk)
            e = jnp.exp(s - m)
            denom = jnp.sum(e, axis=-1, keepdims=True) + jnp.exp(sk - m)
            outs.append(_dot(e.astype(BF16), vh) / denom)
        o_ref[:, t * LANES:(t + 1) * LANES] = jnp.concatenate(outs, axis=-1).astype(o_ref.dtype)


def swa_sink_attention(p64, ppl, sinks, batch, seq):
    n = batch * seq
    nb = seq // BLOCK_Q
    return pl.pallas_call(
        _swa_kernel,
        grid=(batch, nb),
        in_specs=[pl.BlockSpec(memory_space=pltpu.SMEM),
                  pl.BlockSpec((A_HEADS // 2, BLOCK_Q, LANES), lambda b, i: (R64_AQ // 4, b * nb + i, 0)),
                  pl.BlockSpec((1, seq, LANES), lambda b, i: (R64_AK, b, 0)),
                  pl.BlockSpec((1, seq, LANES), lambda b, i: (PL_AV, b, 0))],
        out_specs=pl.BlockSpec((BLOCK_Q, A_HEADS * A_HEAD_DIM), lambda b, i: (b * nb + i, 0)),
        out_shape=jax.ShapeDtypeStruct((n, A_HEADS * A_HEAD_DIM), BF16),
        compiler_params=_cparams(("parallel", "arbitrary")),
        name="swa_sink",
    )(sinks, p64, p64, ppl)


def _compress_kernel(t_ref, pe_ref, w1_ref, w2_ref, o_ref):
    t = t_ref[0, 0]
    half = t.shape[1]
    lo = _dot(t, w1_ref[0:half, :])
    hi = _dot(t, w1_ref[half:2 * half, :])
    rows = t.shape[0]
    hi = pltpu.roll(hi, rows - 1, 0)
    pe = jnp.broadcast_to(pe_ref[...], (8, 2 * half))
    pc = _dot(pe, w1_ref[...])[0:1, :]
    hid = lo + hi + pc
    hid = hid * _sigmoid(hid)
    o_ref[0, 0] = _dot(hid.astype(BF16), w2_ref[...]).astype(o_ref.dtype)


def compress_blocks(tiles, pos_emb, w1, w2, batch, seq):
    g = tiles.shape[0]
    rows = seq // CMP_STRIDE
    t2 = tiles.reshape(g, batch, rows, CMP_STRIDE * LANES)
    pe = pos_emb.reshape(1, CMP_LEN * LANES).astype(BF16)
    return pl.pallas_call(
        _compress_kernel,
        grid=(g, batch),
        in_specs=[pl.BlockSpec((1, 1, rows, CMP_STRIDE * LANES), lambda i, b: (i, b, 0, 0)),
                  pl.BlockSpec((1, CMP_LEN * LANES), lambda i, b: (0, 0)),
                  pl.BlockSpec((CMP_LEN * LANES, LANES), lambda i, b: (0, 0)),
                  pl.BlockSpec((LANES, LANES), lambda i, b: (0, 0))],
        out_specs=pl.BlockSpec((1, 1, rows, LANES), lambda i, b: (i, b, 0, 0)),
        out_shape=jax.ShapeDtypeStruct((g, batch, rows, LANES), BF16),
        compiler_params=_cparams(("parallel", "parallel")),
        name="compress_blocks",
    )(t2, pe, w1.astype(BF16), w2.astype(BF16))


def _nsa_kernel(q_ref, kc_ref, vc_ref, ks_ref, vs_ref, kw_ref, vw_ref, gate_ref, ovlt_ref, o_ref,
                s_scr, mx_scr, ls_scr, acc_scr, *, seq):
    n = pl.program_id(2)
    r = B_HEADS // B_KV_GROUPS
    qb = q_ref.shape[1]
    rq = r * qb
    scale = B_HEAD_DIM ** -0.5
    q4 = q_ref[...].reshape(rq, LANES)
    tpos = n * qb + lax.broadcasted_iota(jnp.int32, (qb, LANES), 0)
    lane = lax.broadcasted_iota(jnp.int32, (qb, LANES), 1)

    nc = seq // CMP_STRIDE - 1
    s = (_dot_nt(q4, kc_ref[0, 0]) * scale).reshape(r, qb, LANES)
    ok = (tpos >= lane * CMP_STRIDE + (CMP_LEN - 1)) & (lane < nc)
    s = jnp.where(ok[None], s, NEG_INF)
    m = jnp.max(s, axis=-1, keepdims=True)
    e = jnp.where(ok[None], jnp.exp(s - m), 0.0)
    l = jnp.sum(e, axis=-1, keepdims=True)
    p = e / jnp.where(l > 0.0, l, 1.0)
    o_cmp = _dot_halves(p.reshape(rq, LANES).astype(BF16), vc_ref[0, 0])

    psum = p[0] + p[1] + p[2] + p[3]
    p_hi = psum.astype(BF16)
    p_mid = (psum - p_hi.astype(F32)).astype(BF16)
    p_lo = (psum - p_hi.astype(F32) - p_mid.astype(F32)).astype(BF16)
    ns = seq // SLC_LEN
    ovl_t = ovlt_ref[...]
    imp_t = (_dot_nt(ovl_t, p_hi) + _dot_nt(ovl_t, p_mid) + _dot_nt(ovl_t, p_lo))[0:ns]
    blk = lax.broadcasted_iota(jnp.int32, (ns, qb), 0)
    tq = n * qb + lax.broadcasted_iota(jnp.int32, (ns, qb), 1)
    cur = tq >> 6
    forced = (blk == 0) | (blk == cur) | (blk == cur - 1)
    future = blk * SLC_LEN > tq
    key = jnp.where(future, -1.0, jnp.where(forced, FORCE_SCORE, imp_t))
    rank = jnp.zeros((ns, qb), F32)
    for i in range(ns):
        ki = key[i:i + 1, :]
        rank = rank + jnp.where(blk > i, jnp.where(ki >= key, 1.0, 0.0), jnp.where(ki > key, 1.0, 0.0))
    sel_t = jnp.where(rank < float(min(SLC_TOPN, ns)), 1.0, 0.0)
    sel_t = jnp.concatenate([sel_t, jnp.zeros((LANES - ns, qb), F32)], axis=0)
    selm = sel_t.T.astype(BF16)

    chunk = s_scr.shape[2]
    halves = chunk // LANES
    mx_scr[...] = jnp.full(mx_scr.shape, NEG_INF, F32)
    ls_scr[...] = jnp.zeros(ls_scr.shape, F32)
    acc_scr[...] = jnp.zeros(acc_scr.shape, F32)
    blk_of_key = lax.broadcasted_iota(jnp.int32, (LANES, chunk), 1) >> 6
    blk_row = lax.broadcasted_iota(jnp.int32, (LANES, chunk), 0)
    kcol = lax.broadcasted_iota(jnp.int32, (qb, chunk), 1)
    trow = n * qb + lax.broadcasted_iota(jnp.int32, (qb, chunk), 0)
    trips = (n * qb + qb + chunk - 1) // chunk

    def slc_scores(c, carry):
        base = pl.multiple_of(c * chunk, chunk)
        kk = ks_ref[0, pl.ds(base, chunk), :]
        expand = jnp.where(blk_row == blk_of_key + c * (chunk // SLC_LEN), 1.0, 0.0).astype(BF16)
        keep = jnp.where((base + kcol) <= trow, _dot(selm, expand), 0.0) > 0.5
        sc = (_dot_nt(q4, kk) * scale).reshape(r, qb, chunk)
        sc = jnp.where(keep[None], sc, NEG_INF).reshape(rq, chunk)
        s_scr[c] = sc
        mx = mx_scr[...]
        for j in range(halves):
            mx = jnp.maximum(mx, sc[:, j * LANES:(j + 1) * LANES])
        mx_scr[...] = mx
        return carry

    lax.fori_loop(0, trips, slc_scores, 0)
    mx_scr[...] = jnp.broadcast_to(jnp.max(mx_scr[...], axis=-1, keepdims=True), mx_scr.shape)

    def slc_values(c, carry):
        base = pl.multiple_of(c * chunk, chunk)
        vv = vs_ref[0, pl.ds(base, chunk), :]
        sc = s_scr[c]
        mb = mx_scr[...]
        es = [jnp.exp(sc[:, j * LANES:(j + 1) * LANES] - mb) for j in range(halves)]
        ls_scr[...] = ls_scr[...] + functools.reduce(lambda a, b: a + b, es)
        acc_scr[...] = acc_scr[...] + _dot_halves(jnp.concatenate(es, axis=-1).astype(BF16), vv)
        return carry

    lax.fori_loop(0, trips, slc_values, 0)
    o_slc = acc_scr[...] / jnp.sum(ls_scr[...], axis=-1, keepdims=True)

    span = B_WINDOW + qb
    start = pl.multiple_of(jnp.maximum(n - B_WINDOW // qb, 0) * qb, qb)
    rel = (n * qb + lax.broadcasted_iota(jnp.int32, (qb, span), 0)
           - start - lax.broadcasted_iota(jnp.int32, (qb, span), 1))
    sw = (_dot_nt(q4, kw_ref[0, pl.ds(start, span), :]) * scale).reshape(r, qb, span)
    sw = jnp.where((rel >= 0)[None], jnp.where((rel < B_WINDOW)[None], sw, NEG_INF), NEG_INF)
    mw = jnp.max(sw, axis=-1, keepdims=True)
    ew = jnp.exp(sw - mw)
    lw = jnp.sum(ew, axis=-1, keepdims=True).reshape(rq, 1)
    o_win = _dot_halves(ew.reshape(rq, span).astype(BF16), vw_ref[0, pl.ds(start, span), :]) / lw

    gates = _sigmoid(gate_ref[0].astype(F32))
    for hh in range(r):
        rows = slice(hh * qb, (hh + 1) * qb)
        o = (gates[:, 3 * hh:3 * hh + 1] * o_cmp[rows]
             + gates[:, 3 * hh + 1:3 * hh + 2] * o_slc[rows]
             + gates[:, 3 * hh + 2:3 * hh + 3] * o_win[rows])
        o_ref[:, hh * LANES:(hh + 1) * LANES] = o.astype(o_ref.dtype)


def _overlap_matrix_t(seq):
    nc = seq // CMP_STRIDE - 1
    ns = seq // SLC_LEN
    j = jnp.arange(LANES)[:, None]
    c = jnp.arange(LANES)[None, :]
    c_start = c * CMP_STRIDE
    hit = (c_start < (j + 1) * SLC_LEN) & (c_start + CMP_LEN > j * SLC_LEN) & (c < nc) & (j < ns)
    return hit.astype(BF16)


def nsa_attention(p128, ppl, kcmp, vcmp, batch, seq):
    n = batch * seq
    qb = NSA_Q_BLOCK
    nb = seq // qb
    r = B_HEADS // B_KV_GROUPS
    rows = seq // CMP_STRIDE
    tok = lambda b, g, i: b * nb + i
    return pl.pallas_call(
        functools.partial(_nsa_kernel, seq=seq),
        grid=(batch, B_KV_GROUPS, nb),
        in_specs=[pl.BlockSpec((r, qb, LANES), lambda b, g, i: (R128_BQ // r + g, tok(b, g, i), 0)),
                  pl.BlockSpec((1, 1, rows, LANES), lambda b, g, i: (g, b, 0, 0)),
                  pl.BlockSpec((1, 1, rows, LANES), lambda b, g, i: (g, b, 0, 0)),
                  pl.BlockSpec((1, seq, LANES), lambda b, g, i: (R128_KS + g, b, 0)),
                  pl.BlockSpec((1, seq, LANES), lambda b, g, i: (PL_VS + g, b, 0)),
                  pl.BlockSpec((1, seq, LANES), lambda b, g, i: (R128_KW + g, b, 0)),
                  pl.BlockSpec((1, seq, LANES), lambda b, g, i: (PL_VW + g, b, 0)),
                  pl.BlockSpec((1, qb, LANES), lambda b, g, i: (PL_GATE + g, tok(b, g, i), 0)),
                  pl.BlockSpec((LANES, LANES), lambda b, g, i: (0, 0))],
        out_specs=pl.BlockSpec((qb, r * LANES), lambda b, g, i: (tok(b, g, i), g)),
        out_shape=jax.ShapeDtypeStruct((n, B_HEADS * B_HEAD_DIM), BF16),
        scratch_shapes=[pltpu.VMEM((seq // NSA_SLC_CHUNK, r * qb, NSA_SLC_CHUNK), F32),
                        pltpu.VMEM((r * qb, LANES), F32), pltpu.VMEM((r * qb, LANES), F32),
                        pltpu.VMEM((r * qb, LANES), F32)],
        compiler_params=_cparams(("parallel", "parallel", "arbitrary")),
        name="nsa_attention",
    )(p128, kcmp, vcmp, p128, ppl, p128, ppl, ppl, _overlap_matrix_t(seq))


def _diff_block(q_ref, k_ref, v_ref, o_ref, s_scr, slot0, first_query, n_chunks, lam, gain):
    qb = q_ref.shape[1]
    _, rows, chunk = s_scr.shape
    groups = chunk // LANES
    q = q_ref[0] * (C_HEAD_DIM ** -0.5)
    lane = lax.broadcasted_iota(jnp.int32, (qb, LANES), 1)
    zero = jnp.zeros_like(q)
    q2 = jnp.concatenate([jnp.where(lane < C_HEAD_DIM, q, zero), jnp.where(lane >= C_HEAD_DIM, q, zero)], axis=0)

    mx = jnp.full((rows, LANES), NEG_INF, F32)
    for c in range(n_chunks):
        sc = _dot_nt(q2, k_ref[0, c * chunk:(c + 1) * chunk, :])
        if c == n_chunks - 1:
            ahead = (lax.broadcasted_iota(jnp.int32, (rows, chunk), 1)
                     - (lax.broadcasted_iota(jnp.int32, (rows, chunk), 0) & (qb - 1)))
            sc = jnp.where(ahead <= first_query - c * chunk, sc, NEG_INF)
        s_scr[slot0 + c] = sc
        for j in range(groups):
            mx = jnp.maximum(mx, sc[:, j * LANES:(j + 1) * LANES])
    mb = jnp.broadcast_to(jnp.max(mx, axis=-1, keepdims=True), (rows, LANES))

    ls = jnp.zeros((rows, LANES), F32)
    acc = jnp.zeros((rows, LANES), F32)
    for c in range(n_chunks):
        sc = s_scr[slot0 + c]
        es = [jnp.exp(sc[:, j * LANES:(j + 1) * LANES] - mb) for j in range(groups)]
        ls = ls + functools.reduce(lambda a, b: a + b, es)
        acc = acc + _dot_halves(jnp.concatenate(es, axis=-1).astype(BF16),
                                v_ref[0, c * chunk:(c + 1) * chunk, :])
    a = acc / jnp.sum(ls, axis=-1, keepdims=True)
    o = a[0:qb] - lam * a[qb:rows]
    ms = jnp.mean(o * o, axis=-1, keepdims=True)
    o_ref[0, 0] = (o * lax.rsqrt(ms + NORM_EPS) * gain).astype(o_ref.dtype)


def _diff_kernel(lam_ref, qlo_ref, qhi_ref, k_ref, v_ref, g_ref, olo_ref, ohi_ref, s_scr, *, lambda_init):
    i = pl.program_id(2)
    half = pl.num_programs(2)
    qb = qlo_ref.shape[1]
    total, _, chunk = s_scr.shape
    lp = lam_ref[...]
    lam = (jnp.exp(jnp.sum(lp[0:1] * lp[1:2], axis=-1, keepdims=True))
           - jnp.exp(jnp.sum(lp[2:3] * lp[3:4], axis=-1, keepdims=True)) + lambda_init)
    gain = g_ref[...] * (1.0 - lambda_init)
    chunks_lo = lax.shift_right_logical(i * qb + qb + chunk - 1, chunk.bit_length() - 1)
    for n_lo in range(1, (half * qb + chunk - 1) // chunk + 1):
        @pl.when(chunks_lo == n_lo)
        def _():
            _diff_block(qlo_ref, k_ref, v_ref, olo_ref, s_scr, 0, i * qb, n_lo, lam, gain)
            _diff_block(qhi_ref, k_ref, v_ref, ohi_ref, s_scr, n_lo, (2 * half - 1 - i) * qb, total - n_lo,
                        lam, gain)


def _pair_blocks(lo, hi, n, width):
    return jnp.concatenate([lo, jnp.flip(hi, axis=1)], axis=1).reshape(n, width)


def diff_attention(p64, ppl, lam_params, g_sub, lambda_init, batch, seq):
    n = batch * seq
    qb = DIFF_Q_BLOCK
    chunk = 2 * qb
    nb = seq // qb
    half = nb // 2
    width = C_HEADS * 2 * C_HEAD_DIM
    out_spec = pl.BlockSpec((1, 1, qb, LANES), lambda b, h, i: (b, i, 0, h))
    out_shape = jax.ShapeDtypeStruct((batch, half, qb, width), BF16)
    lo, hi = pl.pallas_call(
        functools.partial(_diff_kernel, lambda_init=lambda_init),
        grid=(batch, C_HEADS, half),
        in_specs=[pl.BlockSpec((4, C_HEAD_DIM), lambda b, h, i: (0, 0)),
                  pl.BlockSpec((1, qb, LANES), lambda b, h, i: (R64_CQ + h, b * nb + i, 0)),
                  pl.BlockSpec((1, qb, LANES), lambda b, h, i: (R64_CQ + h, b * nb + nb - 1 - i, 0)),
                  pl.BlockSpec((1, seq, LANES), lambda b, h, i: (R64_CK + h, b, 0)),
                  pl.BlockSpec((1, seq, LANES), lambda b, h, i: (PL_CV + h, b, 0)),
                  pl.BlockSpec((1, LANES), lambda b, h, i: (0, 0))],
        out_specs=[out_spec, out_spec],
        out_shape=[out_shape, out_shape],
        scratch_shapes=[pltpu.VMEM((half + 1, 2 * qb, chunk), F32)],
        compiler_params=_cparams(("parallel", "parallel", "arbitrary")),
        name="diff_attention",
    )(lam_params, p64, p64, p64, ppl, g_sub)
    return _pair_blocks(lo, hi, n, width)


def _merge_kernel(oa_ref, ob_ref, oc_ref, wa_ref, wb_ref, wc_ref, ga_ref, gb_ref, gc_ref, o_ref):
    ya = _dot(oa_ref[...], wa_ref[...])
    yb = _dot(ob_ref[...], wb_ref[...])
    yc = _dot(oc_ref[...], wc_ref[...])
    for c in range(ga_ref.shape[0]):
        cols = slice(c * LANES, (c + 1) * LANES)
        o_ref[:, cols] = (ga_ref[c].astype(F32) * ya[:, cols] + gb_ref[c].astype(F32) * yb[:, cols]
                          + gc_ref[c].astype(F32) * yc[:, cols]).astype(o_ref.dtype)


def merge_branches(o_a, o_b, o_c, w_pa, w_pb, w_pc, gates, *, tm=512, tn=1024):
    n = o_a.shape[0]
    d = w_pa.shape[1]
    tpb = tn // LANES
    per_branch = d // tn
    act = lambda k: pl.BlockSpec((tm, k), lambda i, j: (i, 0))
    wsp = lambda k: pl.BlockSpec((k, tn), lambda i, j: (0, j))
    gsp = lambda br: pl.BlockSpec((tpb, tm, LANES), lambda i, j: (br * per_branch + j, i, 0))
    return pl.pallas_call(
        _merge_kernel,
        grid=(n // tm, d // tn),
        in_specs=[act(o_a.shape[1]), act(o_b.shape[1]), act(o_c.shape[1]),
                  wsp(w_pa.shape[0]), wsp(w_pb.shape[0]), wsp(w_pc.shape[0]),
                  gsp(0), gsp(1), gsp(2)],
        out_specs=pl.BlockSpec((tm, tn), lambda i, j: (i, j)),
        out_shape=jax.ShapeDtypeStruct((n, d), BF16),
        compiler_params=_cparams(("parallel", "arbitrary")),
        name="merge_branches",
    )(o_a, o_b, o_c, w_pa, w_pb, w_pc, gates, gates, gates)


def _matmul_residual_kernel(a_ref, w_ref, x_ref, o_ref):
    o_ref[...] = x_ref[...] + _dot(a_ref[...], w_ref[...])


def matmul_residual(a, w, x, *, tm=512, tn=2048):
    n, k = a.shape
    d = w.shape[1]
    return pl.pallas_call(
        _matmul_residual_kernel,
        grid=(n // tm, d // tn),
        in_specs=[pl.BlockSpec((tm, k), lambda i, j: (i, 0)),
                  pl.BlockSpec((k, tn), lambda i, j: (0, j)),
                  pl.BlockSpec((tm, tn), lambda i, j: (i, j))],
        out_specs=pl.BlockSpec((tm, tn), lambda i, j: (i, j)),
        out_shape=jax.ShapeDtypeStruct((n, d), F32),
        compiler_params=_cparams(("parallel", "arbitrary")),
        name="matmul_residual",
    )(a, w, x)


def _xattn_router_kernel(x_ref, gx_ref, wq_ref, kv_ref, wo_ref, gf_ref, wr_hi_ref, wr_lo_ref, br_ref,
                         x2_ref, route_ref, counts_ref):
    x = x_ref[...]
    ms = jnp.mean(x * x, axis=-1, keepdims=True)
    h = (x * lax.rsqrt(ms + NORM_EPS) * gx_ref[...]).astype(BF16)
    q = _dot(h, wq_ref[...]).astype(BF16)
    scale = X_HEAD_DIM ** -0.5
    outs = []
    for hd in range(X_HEADS):
        s = _dot_nt(q[:, hd * LANES:(hd + 1) * LANES], kv_ref[hd]) * scale
        m = jnp.max(s, axis=-1, keepdims=True)
        e = jnp.exp(s - m)
        l = jnp.sum(e, axis=-1, keepdims=True)
        outs.append((_dot_halves(e.astype(BF16), kv_ref[X_HEADS + hd]) / l).astype(BF16))
    x2 = x + _dot(jnp.concatenate(outs, axis=-1), wo_ref[...])
    x2_ref[...] = x2

    ms2 = jnp.mean(x2 * x2, axis=-1, keepdims=True)
    hn = x2 * lax.rsqrt(ms2 + NORM_EPS) * gf_ref[...]
    hn_hi = hn.astype(BF16)
    hn_lo = (hn - hn_hi.astype(F32)).astype(BF16)
    logits = (_dot_halves(hn_hi, wr_hi_ref[...]) + _dot_halves(hn_hi, wr_lo_ref[...])
              + _dot_halves(hn_lo, wr_hi_ref[...])
              + br_ref[...])

    lane = lax.broadcasted_iota(jnp.int32, logits.shape, 1).astype(F32)
    big = 1e9
    in_g = jnp.where(lane >= N_EXPERTS, jnp.where(lane < N_EXPERTS + N_GROUPS, 1.0, 0.0), 0.0) > 0.5
    lg = jnp.where(in_g, logits, NEG_INF)
    eg = jnp.where(in_g, jnp.exp(lg - jnp.max(lg, axis=-1, keepdims=True)), 0.0)
    pg = eg / jnp.sum(eg, axis=-1, keepdims=True)
    gp = jnp.max(pg, axis=-1, keepdims=True)
    gi = jnp.min(jnp.where(in_g, jnp.where(pg == gp, lane, big), big), axis=-1, keepdims=True) - N_EXPERTS
    first = EXPERTS_PER_GROUP * gi
    in_e = jnp.where(lane >= first, jnp.where(lane < first + EXPERTS_PER_GROUP, 1.0, 0.0), 0.0) > 0.5
    le = jnp.where(in_e, logits, NEG_INF)
    ee = jnp.where(in_e, jnp.exp(le - jnp.max(le, axis=-1, keepdims=True)), 0.0)
    pe = jnp.where(in_e, ee / jnp.sum(ee, axis=-1, keepdims=True), -1.0)
    p1 = jnp.max(pe, axis=-1, keepdims=True)
    i1 = jnp.min(jnp.where(pe == p1, lane, big), axis=-1, keepdims=True)
    pe2 = jnp.where(lane == i1, -1.0, pe)
    p2 = jnp.max(pe2, axis=-1, keepdims=True)
    i2 = jnp.min(jnp.where(pe2 == p2, jnp.where(in_e, jnp.where(lane == i1, big, lane), big), big),
                 axis=-1, keepdims=True)
    tot = p1 + p2
    route_ref[...] = jnp.where(lane == 0.0, i1, jnp.where(lane == 1.0, i2, jnp.where(
        lane == 2.0, gp * p1 / tot, jnp.where(lane == 3.0, gp * p2 / tot, 0.0))))

    first_step = (pl.program_id(0) == 0) & (pl.program_id(1) == 0)

    @pl.when(first_step)
    def _():
        counts_ref[...] = jnp.zeros(counts_ref.shape, F32)

    hits = jnp.where(lane == i1, 1.0, 0.0) + jnp.where(lane == i2, 1.0, 0.0)
    counts_ref[...] = counts_ref[...] + jnp.sum(hits, axis=0, keepdims=True)


def xattn_router(x, g_x, w_xq, kv, w_xo, g_ffn, wr_hi, wr_lo, b_r, batch, seq, n_mem, *, tm=512):
    n, d = x.shape
    per_b = seq // tm
    full = lambda shape: pl.BlockSpec(shape, lambda b, i: (0,) * len(shape))
    row = lambda width: pl.BlockSpec((tm, width), lambda b, i: (b * per_b + i, 0))
    return pl.pallas_call(
        _xattn_router_kernel,
        grid=(batch, per_b),
        in_specs=[row(d), full((1, d)), full((d, X_WIDTH)),
                  pl.BlockSpec((2 * X_HEADS, n_mem, LANES), lambda b, i: (0, b, 0)),
                  full((X_WIDTH, d)), full((1, d)), full((d, LANES)), full((d, LANES)), full((1, LANES))],
        out_specs=[row(d), row(LANES), full((1, LANES))],
        out_shape=[jax.ShapeDtypeStruct((n, d), F32), jax.ShapeDtypeStruct((n, LANES), F32),
                   jax.ShapeDtypeStruct((1, LANES), F32)],
        compiler_params=_cparams(("arbitrary", "arbitrary")),
        name="xattn_router",
    )(x, g_x, w_xq, kv, w_xo, g_ffn, wr_hi, wr_lo, b_r)


def _plan_kernel(route_ref, off_ref, dest_ref, run_scr):
    @pl.when(pl.program_id(0) == 0)
    def _():
        run_scr[...] = jnp.zeros(run_scr.shape, F32)

    tm = route_ref.shape[0]
    r = route_ref[...]
    lane = lax.broadcasted_iota(jnp.int32, r.shape, 1).astype(F32)
    a1 = jnp.where(lane == r[:, 0:1], 1.0, 0.0)
    a2 = jnp.where(lane == r[:, 1:2], 1.0, 0.0)
    hits = a1 + a2
    earlier = (lax.broadcasted_iota(jnp.int32, (tm, tm), 0) > lax.broadcasted_iota(jnp.int32, (tm, tm), 1))
    before = _dot_halves(jnp.where(earlier, 1.0, 0.0).astype(BF16), hits.astype(BF16))
    slot = off_ref[...] + run_scr[...] + before
    d1 = jnp.sum(a1 * slot, axis=-1, keepdims=True)
    d2 = jnp.sum(a2 * slot, axis=-1, keepdims=True)
    dest_ref[...] = jnp.where(lane == 0.0, d1, jnp.where(lane == 1.0, d2, 0.0)).astype(jnp.int32)
    run_scr[...] = run_scr[...] + jnp.sum(hits, axis=0, keepdims=True)


def dispatch_plan(route, offsets, *, tm=512):
    n = route.shape[0]
    return pl.pallas_call(
        _plan_kernel,
        grid=(n // tm,),
        in_specs=[pl.BlockSpec((tm, LANES), lambda i: (i, 0)), pl.BlockSpec((1, LANES), lambda i: (0, 0))],
        out_specs=pl.BlockSpec((tm, LANES), lambda i: (i, 0)),
        out_shape=jax.ShapeDtypeStruct((n, LANES), jnp.int32),
        scratch_shapes=[pltpu.VMEM((1, LANES), F32)],
        compiler_params=_cparams(("arbitrary",)),
        name="dispatch_plan",
    )(route, offsets)


def _row_copy(src, src_row, dst, dst_row, sem):
    return pltpu.make_async_copy(src.at[pl.ds(src_row, 1), :], dst.at[pl.ds(dst_row, 1), :], sem)


def _dispatch_kernel(zrow_ref, zflag_ref, d1_ref, d2_ref, x_ref, xs_ref, zero_scr, sems):
    tm = x_ref.shape[0]
    tile = zero_scr.shape[0]

    @pl.when(pl.program_id(0) == 0)
    def _():
        zero_scr[...] = jnp.zeros(zero_scr.shape, F32)
        fills = [pltpu.make_async_copy(zero_scr, xs_ref.at[pl.ds(pl.multiple_of(zrow_ref[e], 8), tile), :],
                                       sems.at[2]) for e in range(zrow_ref.shape[0])]
        for e, fill in enumerate(fills):
            pl.when(zflag_ref[e] > 0)(fill.start)
        for e, fill in enumerate(fills):
            pl.when(zflag_ref[e] > 0)(fill.wait)

    def issue(r, carry):
        _row_copy(x_ref, r, xs_ref, d1_ref[r], sems.at[0]).start()
        _row_copy(x_ref, r, xs_ref, d2_ref[r], sems.at[1]).start(priority=1)
        return carry

    lax.fori_loop(0, tm, issue, 0, unroll=8)
    pltpu.make_async_copy(x_ref, xs_ref.at[pl.ds(0, tm), :], sems.at[0]).wait()
    pltpu.make_async_copy(x_ref, xs_ref.at[pl.ds(0, tm), :], sems.at[1]).wait()


def dispatch_rows(x, d1, d2, zrows, zflags, rows_total, tile, *, tm=512):
    n, d = x.shape
    return pl.pallas_call(
        _dispatch_kernel,
        grid_spec=pltpu.PrefetchScalarGridSpec(
            num_scalar_prefetch=2,
            grid=(n // tm,),
            in_specs=[pl.BlockSpec((tm,), lambda i, z, f: (i,), memory_space=pltpu.SMEM),
                      pl.BlockSpec((tm,), lambda i, z, f: (i,), memory_space=pltpu.SMEM),
                      pl.BlockSpec((tm, d), lambda i, z, f: (i, 0))],
            out_specs=pl.BlockSpec(memory_space=pl.ANY),
            scratch_shapes=[pltpu.VMEM((tile, d), F32), pltpu.SemaphoreType.DMA((3,))]),
        out_shape=jax.ShapeDtypeStruct((rows_total, d), F32),
        compiler_params=_cparams(("arbitrary",)),
        name="dispatch_rows",
    )(zrows, zflags, d1, d2, x)


def _experts_kernel(te_ref, na_ref, xs_ref, g_ref, wg_ref, wu_ref, wd_ref, y_ref):
    active = pl.program_id(0) < na_ref[0]

    @pl.when(active)
    def _():
        x = xs_ref[...]
        ms = jnp.mean(x * x, axis=-1, keepdims=True)
        hn = (x * lax.rsqrt(ms + NORM_EPS) * g_ref[...]).astype(BF16)
        gate = _dot(hn, wg_ref[0, 0])
        up = _dot(hn, wu_ref[0, 0])
        hid = (gate * _sigmoid(gate) * up).astype(BF16)
        y_ref[...] = _dot(hid, wd_ref[0, 0])

    @pl.when(jnp.logical_not(active))
    def _():
        y_ref[...] = jnp.zeros(y_ref.shape, F32)


def expert_ffn(xs, g_ffn, w_gate, w_up, w_down, layer, tile_expert, n_active, tile):
    rows_total, d = xs.shape
    dff = w_gate.shape[-1]
    last = lambda i, te, na: jnp.minimum(i, na[0] - 1)
    return pl.pallas_call(
        _experts_kernel,
        grid_spec=pltpu.PrefetchScalarGridSpec(
            num_scalar_prefetch=2,
            grid=(rows_total // tile,),
            in_specs=[pl.BlockSpec((tile, d), lambda i, te, na: (last(i, te, na), 0)),
                      pl.BlockSpec((1, d), lambda i, te, na: (0, 0)),
                      pl.BlockSpec((1, 1, d, dff), lambda i, te, na: (layer, te[i], 0, 0)),
                      pl.BlockSpec((1, 1, d, dff), lambda i, te, na: (layer, te[i], 0, 0)),
                      pl.BlockSpec((1, 1, dff, d), lambda i, te, na: (layer, te[i], 0, 0))],
            out_specs=pl.BlockSpec((tile, d), lambda i, te, na: (i, 0))),
        out_shape=jax.ShapeDtypeStruct((rows_total, d), F32),
        compiler_params=_cparams(("arbitrary",)),
        name="expert_ffn",
    )(tile_expert, n_active, xs, g_ffn, w_gate, w_up, w_down)


def _combine_kernel(*refs, final_norm):
    if final_norm:
        d1_ref, d2_ref, x_ref, route_ref, y_ref, g_ref, o_ref, buf, sems = refs
    else:
        d1_ref, d2_ref, x_ref, route_ref, y_ref, o_ref, buf, sems = refs
    tm = x_ref.shape[0]

    def issue(r, carry):
        _row_copy(y_ref, d1_ref[r], buf.at[0], r, sems.at[0]).start()
        _row_copy(y_ref, d2_ref[r], buf.at[1], r, sems.at[1]).start(priority=1)
        return carry

    lax.fori_loop(0, tm, issue, 0, unroll=8)
    pltpu.make_async_copy(y_ref.at[pl.ds(0, tm), :], buf.at[0], sems.at[0]).wait()
    pltpu.make_async_copy(y_ref.at[pl.ds(0, tm), :], buf.at[1], sems.at[1]).wait()
    route = route_ref[...]
    o = x_ref[...] + (route[:, 2:3] * buf[0] + route[:, 3:4] * buf[1])
    if final_norm:
        ms = jnp.mean(o * o, axis=-1, keepdims=True)
        o = o * lax.rsqrt(ms + NORM_EPS) * g_ref[...]
    o_ref[...] = o


def combine_rows(x, route, y, d1, d2, final_gain=None, *, tm=512):
    n, d = x.shape
    in_specs = [pl.BlockSpec((tm,), lambda i: (i,), memory_space=pltpu.SMEM),
                pl.BlockSpec((tm,), lambda i: (i,), memory_space=pltpu.SMEM),
                pl.BlockSpec((tm, d), lambda i: (i, 0)),
                pl.BlockSpec((tm, LANES), lambda i: (i, 0)),
                pl.BlockSpec(memory_space=pl.ANY)]
    args = [d1, d2, x, route, y]
    if final_gain is not None:
        in_specs.append(pl.BlockSpec((1, d), lambda i: (0, 0)))
        args.append(final_gain)
    return pl.pallas_call(
        functools.partial(_combine_kernel, final_norm=final_gain is not None),
        grid=(n // tm,),
        in_specs=in_specs,
        out_specs=pl.BlockSpec((tm, d), lambda i: (i, 0)),
        out_shape=jax.ShapeDtypeStruct((n, d), F32),
        scratch_shapes=[pltpu.VMEM((2, tm, d), F32), pltpu.SemaphoreType.DMA((2,))],
        compiler_params=_cparams(("arbitrary",)),
        name="combine_rows",
    )(*args)


def routed_experts(x, route, counts, g_ffn, w_gate, w_up, w_down, layer, final_gain=None, *, tile=MOE_TILE):
    n, d = x.shape
    rows_total = 2 * n + N_EXPERTS * tile
    n_tiles = rows_total // tile
    cnt = counts[0, :N_EXPERTS].astype(jnp.int32)
    tiles = (cnt + tile - 1) // tile
    ends = jnp.cumsum(tiles)
    starts = ends - tiles
    n_active = ends[-1:]
    tile_ids = jnp.minimum(jnp.arange(n_tiles, dtype=jnp.int32), n_active[0] - 1)
    tile_expert = jnp.sum((tile_ids[:, None] >= ends[None, :]).astype(jnp.int32), axis=1)
    offsets = jnp.pad((starts * tile).astype(F32)[None, :], ((0, 0), (0, LANES - N_EXPERTS)))
    tail = n_active[0] + jnp.arange(N_EXPERTS, dtype=jnp.int32)
    zrows = (jnp.concatenate([jnp.maximum(ends - 1, starts), jnp.minimum(tail, n_tiles - 1)]) * tile).astype(jnp.int32)
    zflags = jnp.concatenate([tiles > 0, tail < n_tiles]).astype(jnp.int32)

    dest = dispatch_plan(route, offsets)
    d1, d2 = dest[:, 0], dest[:, 1]
    xs = dispatch_rows(x, d1, d2, zrows, zflags, rows_total, tile)
    y = expert_ffn(xs, g_ffn, w_gate, w_up, w_down, layer, tile_expert.astype(jnp.int32),
                   n_active.astype(jnp.int32), tile)
    return combine_rows(x, route, y, d1, d2, final_gain)


def _pad_cols(w, width):
    return jnp.pad(w, ((0, 0), (0, width - w.shape[1])))


def _regroup_kernel(w_ref, o64_ref, o128_ref, opl_ref, omg_ref):
    w = w_ref[0]
    cols = lambda lo, hi: w[:, lo:hi]
    o64_ref[...] = jnp.concatenate(
        [cols(_A_Q, _A_K), cols(_A_K, _A_V), cols(_C_Q, _C_K), cols(_C_K, _C_V)], axis=1).astype(BF16)
    o128_ref[...] = jnp.concatenate(
        [cols(_B_Q, _B_KC), cols(_B_KC, _B_VC), cols(_B_KS, _B_VS), cols(_B_KW, _B_VW)], axis=1).astype(BF16)
    per_group = 3 * B_HEADS // B_KV_GROUPS
    pad = jnp.zeros((w.shape[0], LANES - per_group), w.dtype)
    gate_tiles = []
    for g in range(B_KV_GROUPS):
        gate_tiles += [cols(_B_GATE + g * per_group, _B_GATE + (g + 1) * per_group), pad]
    opl_ref[...] = jnp.concatenate(
        [cols(_A_V, _B_Q), cols(_B_VC, _B_KS), cols(_B_VS, _B_KW), cols(_B_VW, _B_GATE)]
        + gate_tiles + [cols(_C_V, _MERGE)], axis=1).astype(BF16)
    omg_ref[...] = cols(_MERGE, _IN_END).astype(BF16)


def _input_weights(w_all, layer, *, tr=128):
    d = w_all.shape[1]
    widths = (R64_TILES * LANES, R128_TILES * LANES, PL_TILES * LANES, _IN_END - _MERGE)
    return pl.pallas_call(
        _regroup_kernel,
        grid=(d // tr,),
        in_specs=[pl.BlockSpec((1, tr, w_all.shape[2]), lambda i: (layer, i, 0))],
        out_specs=[pl.BlockSpec((tr, wd), lambda i: (i, 0)) for wd in widths],
        out_shape=[jax.ShapeDtypeStruct((d, wd), BF16) for wd in widths],
        compiler_params=_cparams(("parallel",)),
        name="regroup_w_in",
    )(w_all)


def kernel(x, mem, positions, g_mix, w_in, sinks_a, cmp_pos_k, cmp_pos_v, phi_k1, phi_k2, phi_v1, phi_v2,
           lq1, lk1, lq2, lk2, g_diff, w_pa, w_pb, w_pc, w_out, g_x, g_mem, w_xq, w_xkv, w_xo,
           g_ffn, w_group, b_group, w_expert, b_expert, w_gate, w_up, w_down, g_final):
    batch, seq, d = x.shape
    n_mem = mem.shape[1]
    n = batch * seq
    xf = x.reshape(n, d)
    memf = mem.reshape(batch * n_mem, d)
    pos_f = positions.reshape(n, 1).astype(F32)
    tab64 = rope_tables(pos_f, A_HEAD_DIM)
    tab128 = rope_tables(pos_f, B_HEAD_DIM)
    row = lambda v: v.reshape(1, -1)
    wg_all, wu_all, wd_all = w_gate.astype(BF16), w_up.astype(BF16), w_down.astype(BF16)

    for l in range(DEPTH):
        lambda_init = 0.8 - 0.6 * math.exp(-0.3 * l)
        w64, w128, wpl, wmg = _input_weights(w_in, l)
        g = row(g_mix[l])
        p64 = norm_proj(xf, g, w64, tm=512, tn=w64.shape[1], rope=64, tables=tab64)
        p128 = norm_proj(xf, g, w128, tm=512, tn=w128.shape[1], rope=128, tables=tab128)
        ppl = norm_proj(xf, g, wpl, tm=512, tn=wpl.shape[1])
        gates = norm_proj(xf, g, wmg, tm=1024, tn=1024, sigmoid_out=True)

        o_a = swa_sink_attention(p64, ppl, sinks_a[l], batch, seq)
        kcmp = compress_blocks(p128[R128_KC:R128_KC + B_KV_GROUPS], cmp_pos_k[l], phi_k1[l], phi_k2[l], batch, seq)
        vcmp = compress_blocks(ppl[PL_VC:PL_VC + B_KV_GROUPS], cmp_pos_v[l], phi_v1[l], phi_v2[l], batch, seq)
        o_b = nsa_attention(p128, ppl, kcmp, vcmp, batch, seq)
        lam_params = jnp.stack([lq1[l], lk1[l], lq2[l], lk2[l]])
        o_c = diff_attention(p64, ppl, lam_params, row(g_diff[l]), lambda_init, batch, seq)

        merged = merge_branches(o_a, o_b, o_c, w_pa[l].astype(BF16), w_pb[l].astype(BF16),
                                w_pc[l].astype(BF16), gates)
        xf = matmul_residual(merged, w_out[l].astype(BF16), xf)

        kv = norm_proj(memf, row(g_mem[l]), w_xkv[l].astype(BF16), tm=512, tn=2 * X_WIDTH)
        w_r = _pad_cols(jnp.concatenate([w_expert[l], w_group[l]], axis=1), LANES)
        wr_hi = w_r.astype(BF16)
        wr_lo = (w_r - wr_hi.astype(F32)).astype(BF16)
        b_r = _pad_cols(jnp.concatenate([b_expert[l], b_group[l]])[None, :], LANES)
        xf, route, counts = xattn_router(xf, row(g_x[l]), w_xq[l].astype(BF16), kv, w_xo[l].astype(BF16),
                                         row(g_ffn[l]), wr_hi, wr_lo, b_r, batch, seq, n_mem)
        xf = routed_experts(xf, route, counts, row(g_ffn[l]), wg_all, wu_all, wd_all, l,
                            final_gain=row(g_final) if l == DEPTH - 1 else None)

    return xf.reshape(batch, seq, d)
```

```python
import functools
import math

import jax
import jax.numpy as jnp
from jax import lax
from jax.experimental import pallas as pl
from jax.experimental.pallas import tpu as pltpu

D_MODEL = 2048
DEPTH = 2
ROPE_THETA = 10000.0
NORM_EPS = 1e-6
BLOCK_Q = 128
NEG_INF = -1e30
FORCE_SCORE = 1e6
SWA_Q_BLOCK = 128
NSA_Q_BLOCK = 512
NSA_SLC_CHUNK = 256
DIFF_Q_BLOCK = 256
MOE_TILE = 256

A_HEADS, A_KV_HEADS, A_HEAD_DIM, A_WINDOW = 8, 2, 64, 128
B_HEADS, B_KV_GROUPS, B_HEAD_DIM = 8, 2, 128
CMP_LEN, CMP_STRIDE, SLC_LEN, SLC_TOPN, B_WINDOW = 32, 16, 64, 8, 512
C_HEADS, C_HEAD_DIM = 4, 64
X_HEADS, X_HEAD_DIM = 4, 128
X_WIDTH = X_HEADS * X_HEAD_DIM
N_GROUPS, EXPERTS_PER_GROUP = 4, 4
N_EXPERTS = N_GROUPS * EXPERTS_PER_GROUP
D_FF_EXPERT = 1024

LANES = 128
VMEM_LIMIT = 52 * 1024 * 1024

_A_Q, _A_K, _A_V = 0, 512, 640
_B_Q, _B_KC, _B_VC, _B_KS, _B_VS, _B_KW, _B_VW, _B_GATE = 768, 1792, 2048, 2304, 2560, 2816, 3072, 3328
_C_Q, _C_K, _C_V, _MERGE, _IN_END = 3352, 3864, 4376, 4888, 11032

R64_AQ, R64_AK, R64_CQ, R64_CK, R64_TILES = 0, 4, 5, 9, 13
R128_BQ, R128_KC, R128_KS, R128_KW, R128_TILES = 0, 8, 10, 12, 14
PL_AV, PL_VC, PL_VS, PL_VW, PL_GATE, PL_CV, PL_TILES = 0, 1, 3, 5, 7, 9, 13

BF16 = jnp.bfloat16
F32 = jnp.float32


def _cparams(sem):
    return pltpu.CompilerParams(dimension_semantics=sem, vmem_limit_bytes=VMEM_LIMIT)


def _dot(a, b):
    return jnp.dot(a, b, preferred_element_type=F32)


def _dot_halves(a, b):
    half = a.shape[0] // 2
    return jnp.concatenate([_dot(a[0:half], b), _dot(a[half:], b)], axis=0)


def _dot_nt(a, b):
    return lax.dot_general(a, b, (((1,), (1,)), ((), ())), preferred_element_type=F32)


def _sigmoid(x):
    return 0.5 * jnp.tanh(0.5 * x) + 0.5


def _rope_table_kernel(pos_ref, inv_ref, sign_ref, cos_ref, sin_ref):
    ang = pos_ref[...] * inv_ref[...]
    cos_ref[...] = jnp.cos(ang)
    sin_ref[...] = jnp.sin(ang) * sign_ref[...]


def rope_tables(pos_f, head_dim):
    n = pos_f.shape[0]
    half = head_dim // 2
    lane = jnp.arange(LANES)
    inv = jnp.power(ROPE_THETA, -(2.0 * (lane % half).astype(F32)) / head_dim)[None, :]
    sign = jnp.where((lane % head_dim) < half, -1.0, 1.0).astype(F32)[None, :]
    tm = 2048
    return pl.pallas_call(
        _rope_table_kernel,
        grid=(n // tm,),
        in_specs=[pl.BlockSpec((tm, 1), lambda i: (i, 0)),
                  pl.BlockSpec((1, LANES), lambda i: (0, 0)),
                  pl.BlockSpec((1, LANES), lambda i: (0, 0))],
        out_specs=[pl.BlockSpec((tm, LANES), lambda i: (i, 0)),
                   pl.BlockSpec((tm, LANES), lambda i: (i, 0))],
        out_shape=[jax.ShapeDtypeStruct((n, LANES), F32)] * 2,
        compiler_params=_cparams(("parallel",)),
        name="rope_tables",
    )(pos_f, inv, sign)


def _norm_proj_kernel(*refs, rope, sigmoid_out):
    if rope:
        x_ref, g_ref, w_ref, cos_ref, sin_ref, o_ref, h_scr = refs
    else:
        x_ref, g_ref, w_ref, o_ref, h_scr = refs

    @pl.when(pl.program_id(1) == 0)
    def _():
        x = x_ref[...]
        ms = jnp.mean(x * x, axis=-1, keepdims=True)
        h_scr[...] = (x * lax.rsqrt(ms + NORM_EPS) * g_ref[...]).astype(BF16)

    acc = _dot(h_scr[...], w_ref[...])
    if rope:
        cos = cos_ref[...]
        sin = sin_ref[...]
        if rope == 64:
            first_half = (lax.broadcasted_iota(jnp.int32, cos.shape, 1) & 63) < 32
    for c in range(o_ref.shape[0]):
        a = acc[:, c * LANES:(c + 1) * LANES]
        if rope == 128:
            a = a * cos + pltpu.roll(a, 64, 1) * sin
        elif rope == 64:
            partner = jnp.where(first_half, pltpu.roll(a, 96, 1), pltpu.roll(a, 32, 1))
            a = a * cos + partner * sin
        if sigmoid_out:
            a = _sigmoid(a)
        o_ref[c] = a.astype(o_ref.dtype)


def norm_proj(x, g, w, *, tm, tn, rope=0, tables=None, sigmoid_out=False):
    n, d = x.shape
    tiles = w.shape[1] // LANES
    tpb = tn // LANES
    in_specs = [pl.BlockSpec((tm, d), lambda i, j: (i, 0)),
                pl.BlockSpec((1, d), lambda i, j: (0, 0)),
                pl.BlockSpec((d, tn), lambda i, j: (0, j))]
    args = [x, g, w]
    if rope:
        in_specs += [pl.BlockSpec((tm, LANES), lambda i, j: (i, 0))] * 2
        args += list(tables)
    return pl.pallas_call(
        functools.partial(_norm_proj_kernel, rope=rope, sigmoid_out=sigmoid_out),
        grid=(n // tm, tiles // tpb),
        in_specs=in_specs,
        out_specs=pl.BlockSpec((tpb, tm, LANES), lambda i, j: (j, i, 0)),
        out_shape=jax.ShapeDtypeStruct((tiles, n, LANES), BF16),
        scratch_shapes=[pltpu.VMEM((tm, d), BF16)],
        compiler_params=_cparams(("parallel", "arbitrary")),
        name="norm_proj",
    )(*args)


def _swa_kernel(sink_ref, q_ref, k_ref, v_ref, o_ref):
    n = pl.program_id(1)
    qb = q_ref.shape[1]
    span = A_WINDOW + qb
    start = pl.multiple_of(jnp.maximum(n * qb - A_WINDOW, 0), A_WINDOW)
    kk = k_ref[0, pl.ds(start, span), :]
    vv = v_ref[0, pl.ds(start, span), :]
    qpos = n * qb + lax.broadcasted_iota(jnp.int32, (qb, span), 0)
    kpos = start + lax.broadcasted_iota(jnp.int32, (qb, span), 1)
    rel = qpos - kpos
    scale = A_HEAD_DIM ** -0.5
    heads_per_kv = A_HEADS // A_KV_HEADS
    for t in range(A_HEADS // 2):
        qt = q_ref[t]
        outs = []
        for hh in range(2):
            h = 2 * t + hh
            g = h // heads_per_kv
            qh = qt[:, hh * 64:(hh + 1) * 64]
            kh = kk[:, g * 64:(g + 1) * 64]
            vh = vv[:, g * 64:(g + 1) * 64]
            s = _dot_nt(qh, kh) * scale
            s = jnp.where(rel >= 0, jnp.where(rel < A_WINDOW, s, NEG_INF), NEG_INF)
            sk = sink_ref[h]
            m = jnp.maximum(jnp.max(s, axis=-1, keepdims=True), sk)
            e = jnp.exp(s - m)
            denom = jnp.sum(e, axis=-1, keepdims=True) + jnp.exp(sk - m)
            outs.append(_dot(e.astype(BF16), vh) / denom)
        o_ref[:, t * LANES:(t + 1) * LANES] = jnp.concatenate(outs, axis=-1).astype(o_ref.dtype)


def swa_sink_attention(p64, ppl, sinks, batch, seq):
    n = batch * seq
    qb = SWA_Q_BLOCK
    nb = seq // qb
    return pl.pallas_call(
        _swa_kernel,
        grid=(batch, nb),
        in_specs=[pl.BlockSpec(memory_space=pltpu.SMEM),
                  pl.BlockSpec((A_HEADS // 2, qb, LANES), lambda b, i: (R64_AQ // 4, b * nb + i, 0)),
                  pl.BlockSpec((1, seq, LANES), lambda b, i: (R64_AK, b, 0)),
                  pl.BlockSpec((1, seq, LANES), lambda b, i: (PL_AV, b, 0))],
        out_specs=pl.BlockSpec((qb, A_HEADS * A_HEAD_DIM), lambda b, i: (b * nb + i, 0)),
        out_shape=jax.ShapeDtypeStruct((n, A_HEADS * A_HEAD_DIM), BF16),
        compiler_params=_cparams(("parallel", "arbitrary")),
        name="swa_sink",
    )(sinks, p64, p64, ppl)


def _compress_kernel(t_ref, pe_ref, w1_ref, w2_ref, o_ref):
    t = t_ref[0, 0]
    half = t.shape[1]
    lo = _dot(t, w1_ref[0:half, :])
    hi = _dot(t, w1_ref[half:2 * half, :])
    rows = t.shape[0]
    hi = pltpu.roll(hi, rows - 1, 0)
    pe = jnp.broadcast_to(pe_ref[...], (8, 2 * half))
    pc = _dot(pe, w1_ref[...])[0:1, :]
    hid = lo + hi + pc
    hid = hid * _sigmoid(hid)
    o_ref[0, 0] = _dot(hid.astype(BF16), w2_ref[...]).astype(o_ref.dtype)


def compress_blocks(tiles, pos_emb, w1, w2, batch, seq):
    g = tiles.shape[0]
    rows = seq // CMP_STRIDE
    t2 = tiles.reshape(g, batch, rows, CMP_STRIDE * LANES)
    pe = pos_emb.reshape(1, CMP_LEN * LANES).astype(BF16)
    return pl.pallas_call(
        _compress_kernel,
        grid=(g, batch),
        in_specs=[pl.BlockSpec((1, 1, rows, CMP_STRIDE * LANES), lambda i, b: (i, b, 0, 0)),
                  pl.BlockSpec((1, CMP_LEN * LANES), lambda i, b: (0, 0)),
                  pl.BlockSpec((CMP_LEN * LANES, LANES), lambda i, b: (0, 0)),
                  pl.BlockSpec((LANES, LANES), lambda i, b: (0, 0))],
        out_specs=pl.BlockSpec((1, 1, rows, LANES), lambda i, b: (i, b, 0, 0)),
        out_shape=jax.ShapeDtypeStruct((g, batch, rows, LANES), BF16),
        compiler_params=_cparams(("parallel", "parallel")),
        name="compress_blocks",
    )(t2, pe, w1.astype(BF16), w2.astype(BF16))


def _nsa_kernel(q_ref, kc_ref, vc_ref, ks_ref, vs_ref, kw_ref, vw_ref, gate_ref, ovlt_ref, o_ref,
                s_scr, mx_scr, ls_scr, acc_scr, *, seq):
    n = pl.program_id(2)
    r = B_HEADS // B_KV_GROUPS
    qb = q_ref.shape[1]
    rq = r * qb
    scale = B_HEAD_DIM ** -0.5
    q4 = q_ref[...].reshape(rq, LANES)
    tpos = n * qb + lax.broadcasted_iota(jnp.int32, (qb, LANES), 0)
    lane = lax.broadcasted_iota(jnp.int32, (qb, LANES), 1)

    nc = seq // CMP_STRIDE - 1
    s = (_dot_nt(q4, kc_ref[0, 0]) * scale).reshape(r, qb, LANES)
    ok = (tpos >= lane * CMP_STRIDE + (CMP_LEN - 1)) & (lane < nc)
    s = jnp.where(ok[None], s, NEG_INF)
    m = jnp.max(s, axis=-1, keepdims=True)
    e = jnp.where(ok[None], jnp.exp(s - m), 0.0)
    l = jnp.sum(e, axis=-1, keepdims=True)
    p = e / jnp.where(l > 0.0, l, 1.0)
    o_cmp = _dot_halves(p.reshape(rq, LANES).astype(BF16), vc_ref[0, 0])

    psum = p[0] + p[1] + p[2] + p[3]
    p_hi = psum.astype(BF16)
    p_mid = (psum - p_hi.astype(F32)).astype(BF16)
    p_lo = (psum - p_hi.astype(F32) - p_mid.astype(F32)).astype(BF16)
    ns = seq // SLC_LEN
    ovl_t = ovlt_ref[...]
    imp_t = (_dot_nt(ovl_t, p_hi) + _dot_nt(ovl_t, p_mid) + _dot_nt(ovl_t, p_lo))[0:ns]
    blk = lax.broadcasted_iota(jnp.int32, (ns, qb), 0)
    tq = n * qb + lax.broadcasted_iota(jnp.int32, (ns, qb), 1)
    cur = tq >> 6
    forced = (blk == 0) | (blk == cur) | (blk == cur - 1)
    future = blk * SLC_LEN > tq
    key = jnp.where(future, -1.0, jnp.where(forced, FORCE_SCORE, imp_t))
    rank = jnp.zeros((ns, qb), F32)
    for i in range(ns):
        ki = key[i:i + 1, :]
        rank = rank + jnp.where(blk > i, jnp.where(ki >= key, 1.0, 0.0), jnp.where(ki > key, 1.0, 0.0))
    sel_t = jnp.where(rank < float(min(SLC_TOPN, ns)), 1.0, 0.0)
    sel_t = jnp.concatenate([sel_t, jnp.zeros((LANES - ns, qb), F32)], axis=0)
    selm = sel_t.T.astype(BF16)

    chunk = s_scr.shape[2]
    halves = chunk // LANES
    mx_scr[...] = jnp.full(mx_scr.shape, NEG_INF, F32)
    ls_scr[...] = jnp.zeros(ls_scr.shape, F32)
    acc_scr[...] = jnp.zeros(acc_scr.shape, F32)
    blk_of_key = lax.broadcasted_iota(jnp.int32, (LANES, chunk), 1) >> 6
    blk_row = lax.broadcasted_iota(jnp.int32, (LANES, chunk), 0)
    kcol = lax.broadcasted_iota(jnp.int32, (qb, chunk), 1)
    trow = n * qb + lax.broadcasted_iota(jnp.int32, (qb, chunk), 0)
    trips = (n * qb + qb + chunk - 1) // chunk

    def slc_scores(c, carry):
        base = pl.multiple_of(c * chunk, chunk)
        kk = ks_ref[0, pl.ds(base, chunk), :]
        expand = jnp.where(blk_row == blk_of_key + c * (chunk // SLC_LEN), 1.0, 0.0).astype(BF16)
        keep = jnp.where((base + kcol) <= trow, _dot(selm, expand), 0.0) > 0.5
        sc = (_dot_nt(q4, kk) * scale).reshape(r, qb, chunk)
        sc = jnp.where(keep[None], sc, NEG_INF).reshape(rq, chunk)
        s_scr[c] = sc
        mx = mx_scr[...]
        for j in range(halves):
            mx = jnp.maximum(mx, sc[:, j * LANES:(j + 1) * LANES])
        mx_scr[...] = mx
        return carry

    lax.fori_loop(0, trips, slc_scores, 0)
    mx_scr[...] = jnp.broadcast_to(jnp.max(mx_scr[...], axis=-1, keepdims=True), mx_scr.shape)

    def slc_values(c, carry):
        base = pl.multiple_of(c * chunk, chunk)
        vv = vs_ref[0, pl.ds(base, chunk), :]
        sc = s_scr[c]
        mb = mx_scr[...]
        es = [jnp.exp(sc[:, j * LANES:(j + 1) * LANES] - mb) for j in range(halves)]
        ls_scr[...] = ls_scr[...] + functools.reduce(lambda a, b: a + b, es)
        acc_scr[...] = acc_scr[...] + _dot_halves(jnp.concatenate(es, axis=-1).astype(BF16), vv)
        return carry

    lax.fori_loop(0, trips, slc_values, 0)
    o_slc = acc_scr[...] / jnp.sum(ls_scr[...], axis=-1, keepdims=True)

    span = B_WINDOW + qb
    start = pl.multiple_of(jnp.maximum(n - B_WINDOW // qb, 0) * qb, qb)
    rel = (n * qb + lax.broadcasted_iota(jnp.int32, (qb, span), 0)
           - start - lax.broadcasted_iota(jnp.int32, (qb, span), 1))
    sw = (_dot_nt(q4, kw_ref[0, pl.ds(start, span), :]) * scale).reshape(r, qb, span)
    sw = jnp.where((rel >= 0)[None], jnp.where((rel < B_WINDOW)[None], sw, NEG_INF), NEG_INF)
    mw = jnp.max(sw, axis=-1, keepdims=True)
    ew = jnp.exp(sw - mw)
    lw = jnp.sum(ew, axis=-1, keepdims=True).reshape(rq, 1)
    o_win = _dot_halves(ew.reshape(rq, span).astype(BF16), vw_ref[0, pl.ds(start, span), :]) / lw

    gates = _sigmoid(gate_ref[0].astype(F32))
    for hh in range(r):
        rows = slice(hh * qb, (hh + 1) * qb)
        o = (gates[:, 3 * hh:3 * hh + 1] * o_cmp[rows]
             + gates[:, 3 * hh + 1:3 * hh + 2] * o_slc[rows]
             + gates[:, 3 * hh + 2:3 * hh + 3] * o_win[rows])
        o_ref[:, hh * LANES:(hh + 1) * LANES] = o.astype(o_ref.dtype)


def _overlap_matrix_t(seq):
    nc = seq // CMP_STRIDE - 1
    ns = seq // SLC_LEN
    j = jnp.arange(LANES)[:, None]
    c = jnp.arange(LANES)[None, :]
    c_start = c * CMP_STRIDE
    hit = (c_start < (j + 1) * SLC_LEN) & (c_start + CMP_LEN > j * SLC_LEN) & (c < nc) & (j < ns)
    return hit.astype(BF16)


def nsa_attention(p128, ppl, kcmp, vcmp, batch, seq):
    n = batch * seq
    qb = NSA_Q_BLOCK
    nb = seq // qb
    r = B_HEADS // B_KV_GROUPS
    rows = seq // CMP_STRIDE
    tok = lambda b, g, i: b * nb + i
    return pl.pallas_call(
        functools.partial(_nsa_kernel, seq=seq),
        grid=(batch, B_KV_GROUPS, nb),
        in_specs=[pl.BlockSpec((r, qb, LANES), lambda b, g, i: (R128_BQ // r + g, tok(b, g, i), 0)),
                  pl.BlockSpec((1, 1, rows, LANES), lambda b, g, i: (g, b, 0, 0)),
                  pl.BlockSpec((1, 1, rows, LANES), lambda b, g, i: (g, b, 0, 0)),
                  pl.BlockSpec((1, seq, LANES), lambda b, g, i: (R128_KS + g, b, 0)),
                  pl.BlockSpec((1, seq, LANES), lambda b, g, i: (PL_VS + g, b, 0)),
                  pl.BlockSpec((1, seq, LANES), lambda b, g, i: (R128_KW + g, b, 0)),
                  pl.BlockSpec((1, seq, LANES), lambda b, g, i: (PL_VW + g, b, 0)),
                  pl.BlockSpec((1, qb, LANES), lambda b, g, i: (PL_GATE + g, tok(b, g, i), 0)),
                  pl.BlockSpec((LANES, LANES), lambda b, g, i: (0, 0))],
        out_specs=pl.BlockSpec((qb, r * LANES), lambda b, g, i: (tok(b, g, i), g)),
        out_shape=jax.ShapeDtypeStruct((n, B_HEADS * B_HEAD_DIM), BF16),
        scratch_shapes=[pltpu.VMEM((seq // NSA_SLC_CHUNK, r * qb, NSA_SLC_CHUNK), F32),
                        pltpu.VMEM((r * qb, LANES), F32), pltpu.VMEM((r * qb, LANES), F32),
                        pltpu.VMEM((r * qb, LANES), F32)],
        compiler_params=_cparams(("parallel", "parallel", "arbitrary")),
        name="nsa_attention",
    )(p128, kcmp, vcmp, p128, ppl, p128, ppl, ppl, _overlap_matrix_t(seq))


def _diff_block(q_ref, k_ref, v_ref, o_ref, s_scr, slot0, first_query, n_chunks, lam, gain):
    qb = q_ref.shape[1]
    _, rows, chunk = s_scr.shape
    groups = chunk // LANES
    q = q_ref[0] * (C_HEAD_DIM ** -0.5)
    lane = lax.broadcasted_iota(jnp.int32, (qb, LANES), 1)
    zero = jnp.zeros_like(q)
    q2 = jnp.concatenate([jnp.where(lane < C_HEAD_DIM, q, zero), jnp.where(lane >= C_HEAD_DIM, q, zero)], axis=0)

    mx = jnp.full((rows, LANES), NEG_INF, F32)
    for c in range(n_chunks):
        sc = _dot_nt(q2, k_ref[0, c * chunk:(c + 1) * chunk, :])
        if c == n_chunks - 1:
            ahead = (lax.broadcasted_iota(jnp.int32, (rows, chunk), 1)
                     - (lax.broadcasted_iota(jnp.int32, (rows, chunk), 0) & (qb - 1)))
            sc = jnp.where(ahead <= first_query - c * chunk, sc, NEG_INF)
        s_scr[slot0 + c] = sc
        for j in range(groups):
            mx = jnp.maximum(mx, sc[:, j * LANES:(j + 1) * LANES])
    mb = jnp.broadcast_to(jnp.max(mx, axis=-1, keepdims=True), (rows, LANES))

    ls = jnp.zeros((rows, LANES), F32)
    acc = jnp.zeros((rows, LANES), F32)
    for c in range(n_chunks):
        sc = s_scr[slot0 + c]
        es = [jnp.exp(sc[:, j * LANES:(j + 1) * LANES] - mb) for j in range(groups)]
        ls = ls + functools.reduce(lambda a, b: a + b, es)
        acc = acc + _dot_halves(jnp.concatenate(es, axis=-1).astype(BF16),
                                v_ref[0, c * chunk:(c + 1) * chunk, :])
    a = acc / jnp.sum(ls, axis=-1, keepdims=True)
    o = a[0:qb] - lam * a[qb:rows]
    ms = jnp.mean(o * o, axis=-1, keepdims=True)
    o_ref[0, 0] = (o * lax.rsqrt(ms + NORM_EPS) * gain).astype(o_ref.dtype)


def _diff_kernel(lam_ref, qlo_ref, qhi_ref, k_ref, v_ref, g_ref, olo_ref, ohi_ref, s_scr, *, lambda_init):
    i = pl.program_id(2)
    half = pl.num_programs(2)
    qb = qlo_ref.shape[1]
    total, _, chunk = s_scr.shape
    lp = lam_ref[...]
    lam = (jnp.exp(jnp.sum(lp[0:1] * lp[1:2], axis=-1, keepdims=True))
           - jnp.exp(jnp.sum(lp[2:3] * lp[3:4], axis=-1, keepdims=True)) + lambda_init)
    gain = g_ref[...] * (1.0 - lambda_init)
    chunks_lo = lax.shift_right_logical(i * qb + qb + chunk - 1, chunk.bit_length() - 1)
    for n_lo in range(1, (half * qb + chunk - 1) // chunk + 1):
        @pl.when(chunks_lo == n_lo)
        def _():
            _diff_block(qlo_ref, k_ref, v_ref, olo_ref, s_scr, 0, i * qb, n_lo, lam, gain)
            _diff_block(qhi_ref, k_ref, v_ref, ohi_ref, s_scr, n_lo, (2 * half - 1 - i) * qb, total - n_lo,
                        lam, gain)


def _pair_blocks(lo, hi, n, width):
    return jnp.concatenate([lo, jnp.flip(hi, axis=1)], axis=1).reshape(n, width)


def diff_attention(p64, ppl, lam_params, g_sub, lambda_init, batch, seq):
    n = batch * seq
    qb = DIFF_Q_BLOCK
    chunk = 2 * qb
    nb = seq // qb
    half = nb // 2
    width = C_HEADS * 2 * C_HEAD_DIM
    out_spec = pl.BlockSpec((1, 1, qb, LANES), lambda b, h, i: (b, i, 0, h))
    out_shape = jax.ShapeDtypeStruct((batch, half, qb, width), BF16)
    lo, hi = pl.pallas_call(
        functools.partial(_diff_kernel, lambda_init=lambda_init),
        grid=(batch, C_HEADS, half),
        in_specs=[pl.BlockSpec((4, C_HEAD_DIM), lambda b, h, i: (0, 0)),
                  pl.BlockSpec((1, qb, LANES), lambda b, h, i: (R64_CQ + h, b * nb + i, 0)),
                  pl.BlockSpec((1, qb, LANES), lambda b, h, i: (R64_CQ + h, b * nb + nb - 1 - i, 0)),
                  pl.BlockSpec((1, seq, LANES), lambda b, h, i: (R64_CK + h, b, 0)),
                  pl.BlockSpec((1, seq, LANES), lambda b, h, i: (PL_CV + h, b, 0)),
                  pl.BlockSpec((1, LANES), lambda b, h, i: (0, 0))],
        out_specs=[out_spec, out_spec],
        out_shape=[out_shape, out_shape],
        scratch_shapes=[pltpu.VMEM((half + 1, 2 * qb, chunk), F32)],
        compiler_params=_cparams(("parallel", "parallel", "arbitrary")),
        name="diff_attention",
    )(lam_params, p64, p64, p64, ppl, g_sub)
    return _pair_blocks(lo, hi, n, width)


def _merge_kernel(oa_ref, ob_ref, oc_ref, wa_ref, wb_ref, wc_ref, ga_ref, gb_ref, gc_ref, o_ref):
    ya = _dot(oa_ref[...], wa_ref[...])
    yb = _dot(ob_ref[...], wb_ref[...])
    yc = _dot(oc_ref[...], wc_ref[...])
    for c in range(ga_ref.shape[0]):
        cols = slice(c * LANES, (c + 1) * LANES)
        o_ref[:, cols] = (ga_ref[c].astype(F32) * ya[:, cols] + gb_ref[c].astype(F32) * yb[:, cols]
                          + gc_ref[c].astype(F32) * yc[:, cols]).astype(o_ref.dtype)


def merge_branches(o_a, o_b, o_c, w_pa, w_pb, w_pc, gates, *, tm=512, tn=1024):
    n = o_a.shape[0]
    d = w_pa.shape[1]
    tpb = tn // LANES
    per_branch = d // tn
    act = lambda k: pl.BlockSpec((tm, k), lambda i, j: (i, 0))
    wsp = lambda k: pl.BlockSpec((k, tn), lambda i, j: (0, j))
    gsp = lambda br: pl.BlockSpec((tpb, tm, LANES), lambda i, j: (br * per_branch + j, i, 0))
    return pl.pallas_call(
        _merge_kernel,
        grid=(n // tm, d // tn),
        in_specs=[act(o_a.shape[1]), act(o_b.shape[1]), act(o_c.shape[1]),
                  wsp(w_pa.shape[0]), wsp(w_pb.shape[0]), wsp(w_pc.shape[0]),
                  gsp(0), gsp(1), gsp(2)],
        out_specs=pl.BlockSpec((tm, tn), lambda i, j: (i, j)),
        out_shape=jax.ShapeDtypeStruct((n, d), BF16),
        compiler_params=_cparams(("parallel", "arbitrary")),
        name="merge_branches",
    )(o_a, o_b, o_c, w_pa, w_pb, w_pc, gates, gates, gates)


def _matmul_residual_kernel(a_ref, w_ref, x_ref, o_ref):
    o_ref[...] = x_ref[...] + _dot(a_ref[...], w_ref[...])


def matmul_residual(a, w, x, *, tm=512, tn=2048):
    n, k = a.shape
    d = w.shape[1]
    return pl.pallas_call(
        _matmul_residual_kernel,
        grid=(n // tm, d // tn),
        in_specs=[pl.BlockSpec((tm, k), lambda i, j: (i, 0)),
                  pl.BlockSpec((k, tn), lambda i, j: (0, j)),
                  pl.BlockSpec((tm, tn), lambda i, j: (i, j))],
        out_specs=pl.BlockSpec((tm, tn), lambda i, j: (i, j)),
        out_shape=jax.ShapeDtypeStruct((n, d), F32),
        compiler_params=_cparams(("parallel", "arbitrary")),
        name="matmul_residual",
    )(a, w, x)


def _xattn_router_kernel(x_ref, gx_ref, wq_ref, kv_ref, wo_ref, gf_ref, wr_hi_ref, wr_lo_ref, br_ref,
                         x2_ref, route_ref, counts_ref):
    x = x_ref[...]
    ms = jnp.mean(x * x, axis=-1, keepdims=True)
    h = (x * lax.rsqrt(ms + NORM_EPS) * gx_ref[...]).astype(BF16)
    q = _dot(h, wq_ref[...]).astype(BF16)
    scale = X_HEAD_DIM ** -0.5
    outs = []
    for hd in range(X_HEADS):
        s = _dot_nt(q[:, hd * LANES:(hd + 1) * LANES], kv_ref[hd]) * scale
        m = jnp.max(s, axis=-1, keepdims=True)
        e = jnp.exp(s - m)
        l = jnp.sum(e, axis=-1, keepdims=True)
        outs.append((_dot_halves(e.astype(BF16), kv_ref[X_HEADS + hd]) / l).astype(BF16))
    x2 = x + _dot(jnp.concatenate(outs, axis=-1), wo_ref[...])
    x2_ref[...] = x2

    ms2 = jnp.mean(x2 * x2, axis=-1, keepdims=True)
    hn = x2 * lax.rsqrt(ms2 + NORM_EPS) * gf_ref[...]
    hn_hi = hn.astype(BF16)
    hn_lo = (hn - hn_hi.astype(F32)).astype(BF16)
    logits = (_dot_halves(hn_hi, wr_hi_ref[...]) + _dot_halves(hn_hi, wr_lo_ref[...])
              + _dot_halves(hn_lo, wr_hi_ref[...])
              + br_ref[...])

    lane = lax.broadcasted_iota(jnp.int32, logits.shape, 1).astype(F32)
    big = 1e9
    in_g = jnp.where(lane >= N_EXPERTS, jnp.where(lane < N_EXPERTS + N_GROUPS, 1.0, 0.0), 0.0) > 0.5
    lg = jnp.where(in_g, logits, NEG_INF)
    eg = jnp.where(in_g, jnp.exp(lg - jnp.max(lg, axis=-1, keepdims=True)), 0.0)
    pg = eg / jnp.sum(eg, axis=-1, keepdims=True)
    gp = jnp.max(pg, axis=-1, keepdims=True)
    gi = jnp.min(jnp.where(in_g, jnp.where(pg == gp, lane, big), big), axis=-1, keepdims=True) - N_EXPERTS
    first = EXPERTS_PER_GROUP * gi
    in_e = jnp.where(lane >= first, jnp.where(lane < first + EXPERTS_PER_GROUP, 1.0, 0.0), 0.0) > 0.5
    le = jnp.where(in_e, logits, NEG_INF)
    ee = jnp.where(in_e, jnp.exp(le - jnp.max(le, axis=-1, keepdims=True)), 0.0)
    pe = jnp.where(in_e, ee / jnp.sum(ee, axis=-1, keepdims=True), -1.0)
    p1 = jnp.max(pe, axis=-1, keepdims=True)
    i1 = jnp.min(jnp.where(pe == p1, lane, big), axis=-1, keepdims=True)
    pe2 = jnp.where(lane == i1, -1.0, pe)
    p2 = jnp.max(pe2, axis=-1, keepdims=True)
    i2 = jnp.min(jnp.where(pe2 == p2, jnp.where(in_e, jnp.where(lane == i1, big, lane), big), big),
                 axis=-1, keepdims=True)
    tot = p1 + p2
    route_ref[...] = jnp.where(lane == 0.0, i1, jnp.where(lane == 1.0, i2, jnp.where(
        lane == 2.0, gp * p1 / tot, jnp.where(lane == 3.0, gp * p2 / tot, 0.0))))

    first_step = (pl.program_id(0) == 0) & (pl.program_id(1) == 0)

    @pl.when(first_step)
    def _():
        counts_ref[...] = jnp.zeros(counts_ref.shape, F32)

    hits = jnp.where(lane == i1, 1.0, 0.0) + jnp.where(lane == i2, 1.0, 0.0)
    counts_ref[...] = counts_ref[...] + jnp.sum(hits, axis=0, keepdims=True)


def xattn_router(x, g_x, w_xq, kv, w_xo, g_ffn, wr_hi, wr_lo, b_r, batch, seq, n_mem, *, tm=512):
    n, d = x.shape
    per_b = seq // tm
    full = lambda shape: pl.BlockSpec(shape, lambda b, i: (0,) * len(shape))
    row = lambda width: pl.BlockSpec((tm, width), lambda b, i: (b * per_b + i, 0))
    return pl.pallas_call(
        _xattn_router_kernel,
        grid=(batch, per_b),
        in_specs=[row(d), full((1, d)), full((d, X_WIDTH)),
                  pl.BlockSpec((2 * X_HEADS, n_mem, LANES), lambda b, i: (0, b, 0)),
                  full((X_WIDTH, d)), full((1, d)), full((d, LANES)), full((d, LANES)), full((1, LANES))],
        out_specs=[row(d), row(LANES), full((1, LANES))],
        out_shape=[jax.ShapeDtypeStruct((n, d), F32), jax.ShapeDtypeStruct((n, LANES), F32),
                   jax.ShapeDtypeStruct((1, LANES), F32)],
        compiler_params=_cparams(("arbitrary", "arbitrary")),
        name="xattn_router",
    )(x, g_x, w_xq, kv, w_xo, g_ffn, wr_hi, wr_lo, b_r)


def _plan_kernel(route_ref, off_ref, dest_ref, run_scr):
    @pl.when(pl.program_id(0) == 0)
    def _():
        run_scr[...] = jnp.zeros(run_scr.shape, F32)

    tm = route_ref.shape[0]
    r = route_ref[...]
    lane = lax.broadcasted_iota(jnp.int32, r.shape, 1).astype(F32)
    a1 = jnp.where(lane == r[:, 0:1], 1.0, 0.0)
    a2 = jnp.where(lane == r[:, 1:2], 1.0, 0.0)
    hits = a1 + a2
    earlier = (lax.broadcasted_iota(jnp.int32, (tm, tm), 0) > lax.broadcasted_iota(jnp.int32, (tm, tm), 1))
    before = _dot_halves(jnp.where(earlier, 1.0, 0.0).astype(BF16), hits.astype(BF16))
    slot = off_ref[...] + run_scr[...] + before
    d1 = jnp.sum(a1 * slot, axis=-1, keepdims=True)
    d2 = jnp.sum(a2 * slot, axis=-1, keepdims=True)
    dest_ref[...] = jnp.where(lane == 0.0, d1, jnp.where(lane == 1.0, d2, 0.0)).astype(jnp.int32)
    run_scr[...] = run_scr[...] + jnp.sum(hits, axis=0, keepdims=True)


def dispatch_plan(route, offsets, *, tm=512):
    n = route.shape[0]
    return pl.pallas_call(
        _plan_kernel,
        grid=(n // tm,),
        in_specs=[pl.BlockSpec((tm, LANES), lambda i: (i, 0)), pl.BlockSpec((1, LANES), lambda i: (0, 0))],
        out_specs=pl.BlockSpec((tm, LANES), lambda i: (i, 0)),
        out_shape=jax.ShapeDtypeStruct((n, LANES), jnp.int32),
        scratch_shapes=[pltpu.VMEM((1, LANES), F32)],
        compiler_params=_cparams(("arbitrary",)),
        name="dispatch_plan",
    )(route, offsets)


def _row_copy(src, src_row, dst, dst_row, sem):
    return pltpu.make_async_copy(src.at[pl.ds(src_row, 1), :], dst.at[pl.ds(dst_row, 1), :], sem)


def _dispatch_kernel(zrow_ref, zflag_ref, d1_ref, d2_ref, x_ref, xs_ref, zero_scr, sems):
    tm = x_ref.shape[0]
    tile = zero_scr.shape[0]

    @pl.when(pl.program_id(0) == 0)
    def _():
        zero_scr[...] = jnp.zeros(zero_scr.shape, F32)
        fills = [pltpu.make_async_copy(zero_scr, xs_ref.at[pl.ds(pl.multiple_of(zrow_ref[e], 8), tile), :],
                                       sems.at[2]) for e in range(zrow_ref.shape[0])]
        for e, fill in enumerate(fills):
            pl.when(zflag_ref[e] > 0)(fill.start)
        for e, fill in enumerate(fills):
            pl.when(zflag_ref[e] > 0)(fill.wait)

    def issue(r, carry):
        _row_copy(x_ref, r, xs_ref, d1_ref[r], sems.at[0]).start()
        _row_copy(x_ref, r, xs_ref, d2_ref[r], sems.at[1]).start(priority=1)
        return carry

    lax.fori_loop(0, tm, issue, 0, unroll=8)
    pltpu.make_async_copy(x_ref, xs_ref.at[pl.ds(0, tm), :], sems.at[0]).wait()
    pltpu.make_async_copy(x_ref, xs_ref.at[pl.ds(0, tm), :], sems.at[1]).wait()


def dispatch_rows(x, d1, d2, zrows, zflags, rows_total, tile, *, tm=512):
    n, d = x.shape
    return pl.pallas_call(
        _dispatch_kernel,
        grid_spec=pltpu.PrefetchScalarGridSpec(
            num_scalar_prefetch=2,
            grid=(n // tm,),
            in_specs=[pl.BlockSpec((tm,), lambda i, z, f: (i,), memory_space=pltpu.SMEM),
                      pl.BlockSpec((tm,), lambda i, z, f: (i,), memory_space=pltpu.SMEM),
                      pl.BlockSpec((tm, d), lambda i, z, f: (i, 0))],
            out_specs=pl.BlockSpec(memory_space=pl.ANY),
            scratch_shapes=[pltpu.VMEM((tile, d), F32), pltpu.SemaphoreType.DMA((3,))]),
        out_shape=jax.ShapeDtypeStruct((rows_total, d), F32),
        compiler_params=_cparams(("arbitrary",)),
        name="dispatch_rows",
    )(zrows, zflags, d1, d2, x)


def _experts_kernel(te_ref, na_ref, xs_ref, g_ref, wg_ref, wu_ref, wd_ref, y_ref):
    active = pl.program_id(0) < na_ref[0]

    @pl.when(active)
    def _():
        x = xs_ref[...]
        ms = jnp.mean(x * x, axis=-1, keepdims=True)
        hn = (x * lax.rsqrt(ms + NORM_EPS) * g_ref[...]).astype(BF16)
        gate = _dot(hn, wg_ref[0, 0])
        up = _dot(hn, wu_ref[0, 0])
        hid = (gate * _sigmoid(gate) * up).astype(BF16)
        y_ref[...] = _dot(hid, wd_ref[0, 0])

    @pl.when(jnp.logical_not(active))
    def _():
        y_ref[...] = jnp.zeros(y_ref.shape, F32)


def expert_ffn(xs, g_ffn, w_gate, w_up, w_down, layer, tile_expert, n_active, tile):
    rows_total, d = xs.shape
    dff = w_gate.shape[-1]
    last = lambda i, te, na: jnp.minimum(i, na[0] - 1)
    return pl.pallas_call(
        _experts_kernel,
        grid_spec=pltpu.PrefetchScalarGridSpec(
            num_scalar_prefetch=2,
            grid=(rows_total // tile,),
            in_specs=[pl.BlockSpec((tile, d), lambda i, te, na: (last(i, te, na), 0)),
                      pl.BlockSpec((1, d), lambda i, te, na: (0, 0)),
                      pl.BlockSpec((1, 1, d, dff), lambda i, te, na: (layer, te[i], 0, 0)),
                      pl.BlockSpec((1, 1, d, dff), lambda i, te, na: (layer, te[i], 0, 0)),
                      pl.BlockSpec((1, 1, dff, d), lambda i, te, na: (layer, te[i], 0, 0))],
            out_specs=pl.BlockSpec((tile, d), lambda i, te, na: (i, 0))),
        out_shape=jax.ShapeDtypeStruct((rows_total, d), F32),
        compiler_params=_cparams(("arbitrary",)),
        name="expert_ffn",
    )(tile_expert, n_active, xs, g_ffn, w_gate, w_up, w_down)


def _combine_kernel(*refs, final_norm):
    if final_norm:
        d1_ref, d2_ref, x_ref, route_ref, y_ref, g_ref, o_ref, buf, sems = refs
    else:
        d1_ref, d2_ref, x_ref, route_ref, y_ref, o_ref, buf, sems = refs
    tm = x_ref.shape[0]

    def issue(r, carry):
        _row_copy(y_ref, d1_ref[r], buf.at[0], r, sems.at[0]).start()
        _row_copy(y_ref, d2_ref[r], buf.at[1], r, sems.at[1]).start(priority=1)
        return carry

    lax.fori_loop(0, tm, issue, 0, unroll=8)
    pltpu.make_async_copy(y_ref.at[pl.ds(0, tm), :], buf.at[0], sems.at[0]).wait()
    pltpu.make_async_copy(y_ref.at[pl.ds(0, tm), :], buf.at[1], sems.at[1]).wait()
    route = route_ref[...]
    o = x_ref[...] + (route[:, 2:3] * buf[0] + route[:, 3:4] * buf[1])
    if final_norm:
        ms = jnp.mean(o * o, axis=-1, keepdims=True)
        o = o * lax.rsqrt(ms + NORM_EPS) * g_ref[...]
    o_ref[...] = o


def combine_rows(x, route, y, d1, d2, final_gain=None, *, tm=512):
    n, d = x.shape
    in_specs = [pl.BlockSpec((tm,), lambda i: (i,), memory_space=pltpu.SMEM),
                pl.BlockSpec((tm,), lambda i: (i,), memory_space=pltpu.SMEM),
                pl.BlockSpec((tm, d), lambda i: (i, 0)),
                pl.BlockSpec((tm, LANES), lambda i: (i, 0)),
                pl.BlockSpec(memory_space=pl.ANY)]
    args = [d1, d2, x, route, y]
    if final_gain is not None:
        in_specs.append(pl.BlockSpec((1, d), lambda i: (0, 0)))
        args.append(final_gain)
    return pl.pallas_call(
        functools.partial(_combine_kernel, final_norm=final_gain is not None),
        grid=(n // tm,),
        in_specs=in_specs,
        out_specs=pl.BlockSpec((tm, d), lambda i: (i, 0)),
        out_shape=jax.ShapeDtypeStruct((n, d), F32),
        scratch_shapes=[pltpu.VMEM((2, tm, d), F32), pltpu.SemaphoreType.DMA((2,))],
        compiler_params=_cparams(("arbitrary",)),
        name="combine_rows",
    )(*args)


def routed_experts(x, route, counts, g_ffn, w_gate, w_up, w_down, layer, final_gain=None, *, tile=MOE_TILE):
    n, d = x.shape
    rows_total = 2 * n + N_EXPERTS * tile
    n_tiles = rows_total // tile
    cnt = counts[0, :N_EXPERTS].astype(jnp.int32)
    tiles = (cnt + tile - 1) // tile
    ends = jnp.cumsum(tiles)
    starts = ends - tiles
    n_active = ends[-1:]
    tile_ids = jnp.minimum(jnp.arange(n_tiles, dtype=jnp.int32), n_active[0] - 1)
    tile_expert = jnp.sum((tile_ids[:, None] >= ends[None, :]).astype(jnp.int32), axis=1)
    offsets = jnp.pad((starts * tile).astype(F32)[None, :], ((0, 0), (0, LANES - N_EXPERTS)))
    tail = n_active[0] + jnp.arange(N_EXPERTS, dtype=jnp.int32)
    zrows = (jnp.concatenate([jnp.maximum(ends - 1, starts), jnp.minimum(tail, n_tiles - 1)]) * tile).astype(jnp.int32)
    zflags = jnp.concatenate([tiles > 0, tail < n_tiles]).astype(jnp.int32)

    dest = dispatch_plan(route, offsets)
    d1, d2 = dest[:, 0], dest[:, 1]
    xs = dispatch_rows(x, d1, d2, zrows, zflags, rows_total, tile)
    y = expert_ffn(xs, g_ffn, w_gate, w_up, w_down, layer, tile_expert.astype(jnp.int32),
                   n_active.astype(jnp.int32), tile)
    return combine_rows(x, route, y, d1, d2, final_gain)


def _pad_cols(w, width):
    return jnp.pad(w, ((0, 0), (0, width - w.shape[1])))


def _regroup_kernel(w_ref, o64_ref, o128_ref, opl_ref, omg_ref):
    w = w_ref[0]
    cols = lambda lo, hi: w[:, lo:hi]
    o64_ref[...] = jnp.concatenate(
        [cols(_A_Q, _A_K), cols(_A_K, _A_V), cols(_C_Q, _C_K), cols(_C_K, _C_V)], axis=1).astype(BF16)
    o128_ref[...] = jnp.concatenate(
        [cols(_B_Q, _B_KC), cols(_B_KC, _B_VC), cols(_B_KS, _B_VS), cols(_B_KW, _B_VW)], axis=1).astype(BF16)
    per_group = 3 * B_HEADS // B_KV_GROUPS
    pad = jnp.zeros((w.shape[0], LANES - per_group), w.dtype)
    gate_tiles = []
    for g in range(B_KV_GROUPS):
        gate_tiles += [cols(_B_GATE + g * per_group, _B_GATE + (g + 1) * per_group), pad]
    opl_ref[...] = jnp.concatenate(
        [cols(_A_V, _B_Q), cols(_B_VC, _B_KS), cols(_B_VS, _B_KW), cols(_B_VW, _B_GATE)]
        + gate_tiles + [cols(_C_V, _MERGE)], axis=1).astype(BF16)
    omg_ref[...] = cols(_MERGE, _IN_END).astype(BF16)


def _input_weights(w_all, layer, *, tr=128):
    d = w_all.shape[1]
    widths = (R64_TILES * LANES, R128_TILES * LANES, PL_TILES * LANES, _IN_END - _MERGE)
    return pl.pallas_call(
        _regroup_kernel,
        grid=(d // tr,),
        in_specs=[pl.BlockSpec((1, tr, w_all.shape[2]), lambda i: (layer, i, 0))],
        out_specs=[pl.BlockSpec((tr, wd), lambda i: (i, 0)) for wd in widths],
        out_shape=[jax.ShapeDtypeStruct((d, wd), BF16) for wd in widths],
        compiler_params=_cparams(("parallel",)),
        name="regroup_w_in",
    )(w_all)


def kernel(x, mem, positions, g_mix, w_in, sinks_a, cmp_pos_k, cmp_pos_v, phi_k1, phi_k2, phi_v1, phi_v2,
           lq1, lk1, lq2, lk2, g_diff, w_pa, w_pb, w_pc, w_out, g_x, g_mem, w_xq, w_xkv, w_xo,
           g_ffn, w_group, b_group, w_expert, b_expert, w_gate, w_up, w_down, g_final):
    batch, seq, d = x.shape
    n_mem = mem.shape[1]
    n = batch * seq
    xf = x.reshape(n, d)
    memf = mem.reshape(batch * n_mem, d)
    pos_f = positions.reshape(n, 1).astype(F32)
    tab64 = rope_tables(pos_f, A_HEAD_DIM)
    tab128 = rope_tables(pos_f, B_HEAD_DIM)
    row = lambda v: v.reshape(1, -1)
    wg_all, wu_all, wd_all = w_gate.astype(BF16), w_up.astype(BF16), w_down.astype(BF16)

    for l in range(DEPTH):
        lambda_init = 0.8 - 0.6 * math.exp(-0.3 * l)
        w64, w128, wpl, wmg = _input_weights(w_in, l)
        g = row(g_mix[l])
        p64 = norm_proj(xf, g, w64, tm=512, tn=w64.shape[1], rope=64, tables=tab64)
        p128 = norm_proj(xf, g, w128, tm=512, tn=w128.shape[1], rope=128, tables=tab128)
        ppl = norm_proj(xf, g, wpl, tm=512, tn=wpl.shape[1])
        gates = norm_proj(xf, g, wmg, tm=1024, tn=1024, sigmoid_out=True)

        o_a = swa_sink_attention(p64, ppl, sinks_a[l], batch, seq)
        kcmp = compress_blocks(p128[R128_KC:R128_KC + B_KV_GROUPS], cmp_pos_k[l], phi_k1[l], phi_k2[l], batch, seq)
        vcmp = compress_blocks(ppl[PL_VC:PL_VC + B_KV_GROUPS], cmp_pos_v[l], phi_v1[l], phi_v2[l], batch, seq)
        o_b = nsa_attention(p128, ppl, kcmp, vcmp, batch, seq)
        lam_params = jnp.stack([lq1[l], lk1[l], lq2[l], lk2[l]])
        o_c = diff_attention(p64, ppl, lam_params, row(g_diff[l]), lambda_init, batch, seq)

        merged = merge_branches(o_a, o_b, o_c, w_pa[l].astype(BF16), w_pb[l].astype(BF16),
                                w_pc[l].astype(BF16), gates)
        xf = matmul_residual(merged, w_out[l].astype(BF16), xf)

        kv = norm_proj(memf, row(g_mem[l]), w_xkv[l].astype(BF16), tm=512, tn=2 * X_WIDTH)
        w_r = _pad_cols(jnp.concatenate([w_expert[l], w_group[l]], axis=1), LANES)
        wr_hi = w_r.astype(BF16)
        wr_lo = (w_r - wr_hi.astype(F32)).astype(BF16)
        b_r = _pad_cols(jnp.concatenate([b_expert[l], b_group[l]])[None, :], LANES)
        xf, route, counts = xattn_router(xf, row(g_x[l]), w_xq[l].astype(BF16), kv, w_xo[l].astype(BF16),
                                         row(g_ffn[l]), wr_hi, wr_lo, b_r, batch, seq, n_mem)
        xf = routed_experts(xf, route, counts, row(g_ffn[l]), wg_all, wu_all, wd_all, l,
                            final_gain=row(g_final) if l == DEPTH - 1 else None)

    return xf.reshape(batch, seq, d)
```

```python
import functools
import math

import jax
import jax.numpy as jnp
from jax import lax
from jax.experimental import pallas as pl
from jax.experimental.pallas import tpu as pltpu

D_MODEL = 2048
DEPTH = 2
ROPE_THETA = 10000.0
NORM_EPS = 1e-6
BLOCK_Q = 128
NEG_INF = -1e30
FORCE_SCORE = 1e6
SWA_Q_BLOCK = 128
NSA_Q_BLOCK = 512
NSA_SLC_CHUNK = 512
DIFF_Q_BLOCK = 256
MOE_TILE = 256

A_HEADS, A_KV_HEADS, A_HEAD_DIM, A_WINDOW = 8, 2, 64, 128
B_HEADS, B_KV_GROUPS, B_HEAD_DIM = 8, 2, 128
CMP_LEN, CMP_STRIDE, SLC_LEN, SLC_TOPN, B_WINDOW = 32, 16, 64, 8, 512
C_HEADS, C_HEAD_DIM = 4, 64
X_HEADS, X_HEAD_DIM = 4, 128
X_WIDTH = X_HEADS * X_HEAD_DIM
N_GROUPS, EXPERTS_PER_GROUP = 4, 4
N_EXPERTS = N_GROUPS * EXPERTS_PER_GROUP
D_FF_EXPERT = 1024

LANES = 128
VMEM_LIMIT = 52 * 1024 * 1024

_A_Q, _A_K, _A_V = 0, 512, 640
_B_Q, _B_KC, _B_VC, _B_KS, _B_VS, _B_KW, _B_VW, _B_GATE = 768, 1792, 2048, 2304, 2560, 2816, 3072, 3328
_C_Q, _C_K, _C_V, _MERGE, _IN_END = 3352, 3864, 4376, 4888, 11032

R64_AQ, R64_AK, R64_CQ, R64_CK, R64_TILES = 0, 4, 5, 9, 13
R128_BQ, R128_KC, R128_KS, R128_KW, R128_TILES = 0, 8, 10, 12, 14
PL_AV, PL_VC, PL_VS, PL_VW, PL_GATE, PL_CV, PL_TILES = 0, 1, 3, 5, 7, 9, 13

BF16 = jnp.bfloat16
F32 = jnp.float32


def _cparams(sem):
    return pltpu.CompilerParams(dimension_semantics=sem, vmem_limit_bytes=VMEM_LIMIT)


def _dot(a, b):
    return jnp.dot(a, b, preferred_element_type=F32)


def _dot_halves(a, b):
    half = a.shape[0] // 2
    return jnp.concatenate([_dot(a[0:half], b), _dot(a[half:], b)], axis=0)


def _dot_nt(a, b):
    return lax.dot_general(a, b, (((1,), (1,)), ((), ())), preferred_element_type=F32)


def _sigmoid(x):
    return 0.5 * jnp.tanh(0.5 * x) + 0.5


def _rope_table_kernel(pos_ref, inv_ref, sign_ref, cos_ref, sin_ref):
    ang = pos_ref[...] * inv_ref[...]
    cos_ref[...] = jnp.cos(ang)
    sin_ref[...] = jnp.sin(ang) * sign_ref[...]


def rope_tables(pos_f, head_dim):
    n = pos_f.shape[0]
    half = head_dim // 2
    lane = jnp.arange(LANES)
    inv = jnp.power(ROPE_THETA, -(2.0 * (lane % half).astype(F32)) / head_dim)[None, :]
    sign = jnp.where((lane % head_dim) < half, -1.0, 1.0).astype(F32)[None, :]
    tm = 2048
    return pl.pallas_call(
        _rope_table_kernel,
        grid=(n // tm,),
        in_specs=[pl.BlockSpec((tm, 1), lambda i: (i, 0)),
                  pl.BlockSpec((1, LANES), lambda i: (0, 0)),
                  pl.BlockSpec((1, LANES), lambda i: (0, 0))],
        out_specs=[pl.BlockSpec((tm, LANES), lambda i: (i, 0)),
                   pl.BlockSpec((tm, LANES), lambda i: (i, 0))],
        out_shape=[jax.ShapeDtypeStruct((n, LANES), F32)] * 2,
        compiler_params=_cparams(("parallel",)),
        name="rope_tables",
    )(pos_f, inv, sign)


def _norm_proj_kernel(*refs, rope, sigmoid_out):
    if rope:
        x_ref, g_ref, w_ref, cos_ref, sin_ref, o_ref, h_scr = refs
    else:
        x_ref, g_ref, w_ref, o_ref, h_scr = refs

    @pl.when(pl.program_id(1) == 0)
    def _():
        x = x_ref[...]
        ms = jnp.mean(x * x, axis=-1, keepdims=True)
        h_scr[...] = (x * lax.rsqrt(ms + NORM_EPS) * g_ref[...]).astype(BF16)

    acc = _dot(h_scr[...], w_ref[...])
    if rope:
        cos = cos_ref[...]
        sin = sin_ref[...]
        if rope == 64:
            first_half = (lax.broadcasted_iota(jnp.int32, cos.shape, 1) & 63) < 32
    for c in range(o_ref.shape[0]):
        a = acc[:, c * LANES:(c + 1) * LANES]
        if rope == 128:
            a = a * cos + pltpu.roll(a, 64, 1) * sin
        elif rope == 64:
            partner = jnp.where(first_half, pltpu.roll(a, 96, 1), pltpu.roll(a, 32, 1))
            a = a * cos + partner * sin
        if sigmoid_out:
            a = _sigmoid(a)
        o_ref[c] = a.astype(o_ref.dtype)


def norm_proj(x, g, w, *, tm, tn, rope=0, tables=None, sigmoid_out=False):
    n, d = x.shape
    tiles = w.shape[1] // LANES
    tpb = tn // LANES
    in_specs = [pl.BlockSpec((tm, d), lambda i, j: (i, 0)),
                pl.BlockSpec((1, d), lambda i, j: (0, 0)),
                pl.BlockSpec((d, tn), lambda i, j: (0, j))]
    args = [x, g, w]
    if rope:
        in_specs += [pl.BlockSpec((tm, LANES), lambda i, j: (i, 0))] * 2
        args += list(tables)
    return pl.pallas_call(
        functools.partial(_norm_proj_kernel, rope=rope, sigmoid_out=sigmoid_out),
        grid=(n // tm, tiles // tpb),
        in_specs=in_specs,
        out_specs=pl.BlockSpec((tpb, tm, LANES), lambda i, j: (j, i, 0)),
        out_shape=jax.ShapeDtypeStruct((tiles, n, LANES), BF16),
        scratch_shapes=[pltpu.VMEM((tm, d), BF16)],
        compiler_params=_cparams(("parallel", "arbitrary")),
        name="norm_proj",
    )(*args)


def _swa_kernel(sink_ref, q_ref, k_ref, v_ref, o_ref):
    n = pl.program_id(1)
    qb = q_ref.shape[1]
    span = A_WINDOW + qb
    start = pl.multiple_of(jnp.maximum(n * qb - A_WINDOW, 0), A_WINDOW)
    kk = k_ref[0, pl.ds(start, span), :]
    vv = v_ref[0, pl.ds(start, span), :]
    qpos = n * qb + lax.broadcasted_iota(jnp.int32, (qb, span), 0)
    kpos = start + lax.broadcasted_iota(jnp.int32, (qb, span), 1)
    rel = qpos - kpos
    scale = A_HEAD_DIM ** -0.5
    heads_per_kv = A_HEADS // A_KV_HEADS
    for t in range(A_HEADS // 2):
        qt = q_ref[t]
        outs = []
        for hh in range(2):
            h = 2 * t + hh
            g = h // heads_per_kv
            qh = qt[:, hh * 64:(hh + 1) * 64]
            kh = kk[:, g * 64:(g + 1) * 64]
            vh = vv[:, g * 64:(g + 1) * 64]
            s = _dot_nt(qh, kh) * scale
            s = jnp.where(rel >= 0, jnp.where(rel < A_WINDOW, s, NEG_INF), NEG_INF)
            sk = sink_ref[h]
            m = jnp.maximum(jnp.max(s, axis=-1, keepdims=True), sk)
            e = jnp.exp(s - m)
            denom = jnp.sum(e, axis=-1, keepdims=True) + jnp.exp(sk - m)
            outs.append(_dot(e.astype(BF16), vh) / denom)
        o_ref[:, t * LANES:(t + 1) * LANES] = jnp.concatenate(outs, axis=-1).astype(o_ref.dtype)


def swa_sink_attention(p64, ppl, sinks, batch, seq):
    n = batch * seq
    qb = SWA_Q_BLOCK
    nb = seq // qb
    return pl.pallas_call(
        _swa_kernel,
        grid=(batch, nb),
        in_specs=[pl.BlockSpec(memory_space=pltpu.SMEM),
                  pl.BlockSpec((A_HEADS // 2, qb, LANES), lambda b, i: (R64_AQ // 4, b * nb + i, 0)),
                  pl.BlockSpec((1, seq, LANES), lambda b, i: (R64_AK, b, 0)),
                  pl.BlockSpec((1, seq, LANES), lambda b, i: (PL_AV, b, 0))],
        out_specs=pl.BlockSpec((qb, A_HEADS * A_HEAD_DIM), lambda b, i: (b * nb + i, 0)),
        out_shape=jax.ShapeDtypeStruct((n, A_HEADS * A_HEAD_DIM), BF16),
        compiler_params=_cparams(("parallel", "arbitrary")),
        name="swa_sink",
    )(sinks, p64, p64, ppl)


def _compress_kernel(t_ref, pe_ref, w1_ref, w2_ref, o_ref):
    t = t_ref[0, 0]
    half = t.shape[1]
    lo = _dot(t, w1_ref[0:half, :])
    hi = _dot(t, w1_ref[half:2 * half, :])
    rows = t.shape[0]
    hi = pltpu.roll(hi, rows - 1, 0)
    pe = jnp.broadcast_to(pe_ref[...], (8, 2 * half))
    pc = _dot(pe, w1_ref[...])[0:1, :]
    hid = lo + hi + pc
    hid = hid * _sigmoid(hid)
    o_ref[0, 0] = _dot(hid.astype(BF16), w2_ref[...]).astype(o_ref.dtype)


def compress_blocks(tiles, pos_emb, w1, w2, batch, seq):
    g = tiles.shape[0]
    rows = seq // CMP_STRIDE
    t2 = tiles.reshape(g, batch, rows, CMP_STRIDE * LANES)
    pe = pos_emb.reshape(1, CMP_LEN * LANES).astype(BF16)
    return pl.pallas_call(
        _compress_kernel,
        grid=(g, batch),
        in_specs=[pl.BlockSpec((1, 1, rows, CMP_STRIDE * LANES), lambda i, b: (i, b, 0, 0)),
                  pl.BlockSpec((1, CMP_LEN * LANES), lambda i, b: (0, 0)),
                  pl.BlockSpec((CMP_LEN * LANES, LANES), lambda i, b: (0, 0)),
                  pl.BlockSpec((LANES, LANES), lambda i, b: (0, 0))],
        out_specs=pl.BlockSpec((1, 1, rows, LANES), lambda i, b: (i, b, 0, 0)),
        out_shape=jax.ShapeDtypeStruct((g, batch, rows, LANES), BF16),
        compiler_params=_cparams(("parallel", "parallel")),
        name="compress_blocks",
    )(t2, pe, w1.astype(BF16), w2.astype(BF16))


def _nsa_kernel(q_ref, kc_ref, vc_ref, ks_ref, vs_ref, kw_ref, vw_ref, gate_ref, ovlt_ref, o_ref,
                s_scr, mx_scr, ls_scr, acc_scr, *, seq):
    n = pl.program_id(2)
    r = B_HEADS // B_KV_GROUPS
    qb = q_ref.shape[1]
    rq = r * qb
    scale = B_HEAD_DIM ** -0.5
    q4 = q_ref[...].reshape(rq, LANES)
    tpos = n * qb + lax.broadcasted_iota(jnp.int32, (qb, LANES), 0)
    lane = lax.broadcasted_iota(jnp.int32, (qb, LANES), 1)

    nc = seq // CMP_STRIDE - 1
    s = (_dot_nt(q4, kc_ref[0, 0]) * scale).reshape(r, qb, LANES)
    ok = (tpos >= lane * CMP_STRIDE + (CMP_LEN - 1)) & (lane < nc)
    s = jnp.where(ok[None], s, NEG_INF)
    m = jnp.max(s, axis=-1, keepdims=True)
    e = jnp.where(ok[None], jnp.exp(s - m), 0.0)
    l = jnp.sum(e, axis=-1, keepdims=True)
    p = e / jnp.where(l > 0.0, l, 1.0)
    o_cmp = _dot_halves(p.reshape(rq, LANES).astype(BF16), vc_ref[0, 0])

    psum = p[0] + p[1] + p[2] + p[3]
    p_hi = psum.astype(BF16)
    p_mid = (psum - p_hi.astype(F32)).astype(BF16)
    p_lo = (psum - p_hi.astype(F32) - p_mid.astype(F32)).astype(BF16)
    ns = seq // SLC_LEN
    ovl_t = ovlt_ref[...]
    imp_t = (_dot_nt(ovl_t, p_hi) + _dot_nt(ovl_t, p_mid) + _dot_nt(ovl_t, p_lo))[0:ns]
    blk = lax.broadcasted_iota(jnp.int32, (ns, qb), 0)
    tq = n * qb + lax.broadcasted_iota(jnp.int32, (ns, qb), 1)
    cur = tq >> 6
    forced = (blk == 0) | (blk == cur) | (blk == cur - 1)
    future = blk * SLC_LEN > tq
    key = jnp.where(future, -1.0, jnp.where(forced, FORCE_SCORE, imp_t))
    rank = jnp.zeros((ns, qb), F32)
    for i in range(ns):
        ki = key[i:i + 1, :]
        rank = rank + jnp.where(blk > i, jnp.where(ki >= key, 1.0, 0.0), jnp.where(ki > key, 1.0, 0.0))
    sel_t = jnp.where(rank < float(min(SLC_TOPN, ns)), 1.0, 0.0)
    sel_t = jnp.concatenate([sel_t, jnp.zeros((LANES - ns, qb), F32)], axis=0)
    selm = sel_t.T.astype(BF16)

    chunk = s_scr.shape[2]
    halves = chunk // LANES
    mx_scr[...] = jnp.full(mx_scr.shape, NEG_INF, F32)
    ls_scr[...] = jnp.zeros(ls_scr.shape, F32)
    acc_scr[...] = jnp.zeros(acc_scr.shape, F32)
    blk_of_key = lax.broadcasted_iota(jnp.int32, (LANES, chunk), 1) >> 6
    blk_row = lax.broadcasted_iota(jnp.int32, (LANES, chunk), 0)
    kcol = lax.broadcasted_iota(jnp.int32, (qb, chunk), 1)
    trow = n * qb + lax.broadcasted_iota(jnp.int32, (qb, chunk), 0)
    trips = (n * qb + qb + chunk - 1) // chunk

    def slc_scores(c, carry):
        base = pl.multiple_of(c * chunk, chunk)
        kk = ks_ref[0, pl.ds(base, chunk), :]
        expand = jnp.where(blk_row == blk_of_key + c * (chunk // SLC_LEN), 1.0, 0.0).astype(BF16)
        keep = jnp.where((base + kcol) <= trow, _dot(selm, expand), 0.0) > 0.5
        sc = (_dot_nt(q4, kk) * scale).reshape(r, qb, chunk)
        sc = jnp.where(keep[None], sc, NEG_INF).reshape(rq, chunk)
        s_scr[c] = sc
        mx = mx_scr[...]
        for j in range(halves):
            mx = jnp.maximum(mx, sc[:, j * LANES:(j + 1) * LANES])
        mx_scr[...] = mx
        return carry

    lax.fori_loop(0, trips, slc_scores, 0)
    mx_scr[...] = jnp.broadcast_to(jnp.max(mx_scr[...], axis=-1, keepdims=True), mx_scr.shape)

    def slc_values(c, carry):
        base = pl.multiple_of(c * chunk, chunk)
        vv = vs_ref[0, pl.ds(base, chunk), :]
        sc = s_scr[c]
        mb = mx_scr[...]
        es = [jnp.exp(sc[:, j * LANES:(j + 1) * LANES] - mb) for j in range(halves)]
        ls_scr[...] = ls_scr[...] + functools.reduce(lambda a, b: a + b, es)
        acc_scr[...] = acc_scr[...] + _dot_halves(jnp.concatenate(es, axis=-1).astype(BF16), vv)
        return carry

    lax.fori_loop(0, trips, slc_values, 0)
    o_slc = acc_scr[...] / jnp.sum(ls_scr[...], axis=-1, keepdims=True)

    span = B_WINDOW + qb
    start = pl.multiple_of(jnp.maximum(n - B_WINDOW // qb, 0) * qb, qb)
    rel = (n * qb + lax.broadcasted_iota(jnp.int32, (qb, span), 0)
           - start - lax.broadcasted_iota(jnp.int32, (qb, span), 1))
    sw = (_dot_nt(q4, kw_ref[0, pl.ds(start, span), :]) * scale).reshape(r, qb, span)
    sw = jnp.where((rel >= 0)[None], jnp.where((rel < B_WINDOW)[None], sw, NEG_INF), NEG_INF)
    mw = jnp.max(sw, axis=-1, keepdims=True)
    ew = jnp.exp(sw - mw)
    lw = jnp.sum(ew, axis=-1, keepdims=True).reshape(rq, 1)
    o_win = _dot_halves(ew.reshape(rq, span).astype(BF16), vw_ref[0, pl.ds(start, span), :]) / lw

    gates = _sigmoid(gate_ref[0].astype(F32))
    for hh in range(r):
        rows = slice(hh * qb, (hh + 1) * qb)
        o = (gates[:, 3 * hh:3 * hh + 1] * o_cmp[rows]
             + gates[:, 3 * hh + 1:3 * hh + 2] * o_slc[rows]
             + gates[:, 3 * hh + 2:3 * hh + 3] * o_win[rows])
        o_ref[:, hh * LANES:(hh + 1) * LANES] = o.astype(o_ref.dtype)


def _overlap_matrix_t(seq):
    nc = seq // CMP_STRIDE - 1
    ns = seq // SLC_LEN
    j = jnp.arange(LANES)[:, None]
    c = jnp.arange(LANES)[None, :]
    c_start = c * CMP_STRIDE
    hit = (c_start < (j + 1) * SLC_LEN) & (c_start + CMP_LEN > j * SLC_LEN) & (c < nc) & (j < ns)
    return hit.astype(BF16)


def nsa_attention(p128, ppl, kcmp, vcmp, batch, seq):
    n = batch * seq
    qb = NSA_Q_BLOCK
    nb = seq // qb
    r = B_HEADS // B_KV_GROUPS
    rows = seq // CMP_STRIDE
    tok = lambda b, g, i: b * nb + i
    return pl.pallas_call(
        functools.partial(_nsa_kernel, seq=seq),
        grid=(batch, B_KV_GROUPS, nb),
        in_specs=[pl.BlockSpec((r, qb, LANES), lambda b, g, i: (R128_BQ // r + g, tok(b, g, i), 0)),
                  pl.BlockSpec((1, 1, rows, LANES), lambda b, g, i: (g, b, 0, 0)),
                  pl.BlockSpec((1, 1, rows, LANES), lambda b, g, i: (g, b, 0, 0)),
                  pl.BlockSpec((1, seq, LANES), lambda b, g, i: (R128_KS + g, b, 0)),
                  pl.BlockSpec((1, seq, LANES), lambda b, g, i: (PL_VS + g, b, 0)),
                  pl.BlockSpec((1, seq, LANES), lambda b, g, i: (R128_KW + g, b, 0)),
                  pl.BlockSpec((1, seq, LANES), lambda b, g, i: (PL_VW + g, b, 0)),
                  pl.BlockSpec((1, qb, LANES), lambda b, g, i: (PL_GATE + g, tok(b, g, i), 0)),
                  pl.BlockSpec((LANES, LANES), lambda b, g, i: (0, 0))],
        out_specs=pl.BlockSpec((qb, r * LANES), lambda b, g, i: (tok(b, g, i), g)),
        out_shape=jax.ShapeDtypeStruct((n, B_HEADS * B_HEAD_DIM), BF16),
        scratch_shapes=[pltpu.VMEM((seq // NSA_SLC_CHUNK, r * qb, NSA_SLC_CHUNK), F32),
                        pltpu.VMEM((r * qb, LANES), F32), pltpu.VMEM((r * qb, LANES), F32),
                        pltpu.VMEM((r * qb, LANES), F32)],
        compiler_params=_cparams(("parallel", "parallel", "arbitrary")),
        name="nsa_attention",
    )(p128, kcmp, vcmp, p128, ppl, p128, ppl, ppl, _overlap_matrix_t(seq))


def _diff_block(q_ref, k_ref, v_ref, o_ref, s_scr, slot0, first_query, n_chunks, lam, gain):
    qb = q_ref.shape[1]
    _, rows, chunk = s_scr.shape
    groups = chunk // LANES
    q = q_ref[0] * (C_HEAD_DIM ** -0.5)
    lane = lax.broadcasted_iota(jnp.int32, (qb, LANES), 1)
    zero = jnp.zeros_like(q)
    q2 = jnp.concatenate([jnp.where(lane < C_HEAD_DIM, q, zero), jnp.where(lane >= C_HEAD_DIM, q, zero)], axis=0)

    mx = jnp.full((rows, LANES), NEG_INF, F32)
    for c in range(n_chunks):
        sc = _dot_nt(q2, k_ref[0, c * chunk:(c + 1) * chunk, :])
        if c == n_chunks - 1:
            ahead = (lax.broadcasted_iota(jnp.int32, (rows, chunk), 1)
                     - (lax.broadcasted_iota(jnp.int32, (rows, chunk), 0) & (qb - 1)))
            sc = jnp.where(ahead <= first_query - c * chunk, sc, NEG_INF)
        s_scr[slot0 + c] = sc
        for j in range(groups):
            mx = jnp.maximum(mx, sc[:, j * LANES:(j + 1) * LANES])
    mb = jnp.broadcast_to(jnp.max(mx, axis=-1, keepdims=True), (rows, LANES))

    ls = jnp.zeros((rows, LANES), F32)
    acc = jnp.zeros((rows, LANES), F32)
    for c in range(n_chunks):
        sc = s_scr[slot0 + c]
        es = [jnp.exp(sc[:, j * LANES:(j + 1) * LANES] - mb) for j in range(groups)]
        ls = ls + functools.reduce(lambda a, b: a + b, es)
        acc = acc + _dot_halves(jnp.concatenate(es, axis=-1).astype(BF16),
                                v_ref[0, c * chunk:(c + 1) * chunk, :])
    a = acc / jnp.sum(ls, axis=-1, keepdims=True)
    o = a[0:qb] - lam * a[qb:rows]
    ms = jnp.mean(o * o, axis=-1, keepdims=True)
    o_ref[0, 0] = (o * lax.rsqrt(ms + NORM_EPS) * gain).astype(o_ref.dtype)


def _diff_kernel(lam_ref, qlo_ref, qhi_ref, k_ref, v_ref, g_ref, olo_ref, ohi_ref, s_scr, *, lambda_init):
    i = pl.program_id(2)
    half = pl.num_programs(2)
    qb = qlo_ref.shape[1]
    total, _, chunk = s_scr.shape
    lp = lam_ref[...]
    lam = (jnp.exp(jnp.sum(lp[0:1] * lp[1:2], axis=-1, keepdims=True))
           - jnp.exp(jnp.sum(lp[2:3] * lp[3:4], axis=-1, keepdims=True)) + lambda_init)
    gain = g_ref[...] * (1.0 - lambda_init)
    chunks_lo = lax.shift_right_logical(i * qb + qb + chunk - 1, chunk.bit_length() - 1)
    for n_lo in range(1, (half * qb + chunk - 1) // chunk + 1):
        @pl.when(chunks_lo == n_lo)
        def _():
            _diff_block(qlo_ref, k_ref, v_ref, olo_ref, s_scr, 0, i * qb, n_lo, lam, gain)
            _diff_block(qhi_ref, k_ref, v_ref, ohi_ref, s_scr, n_lo, (2 * half - 1 - i) * qb, total - n_lo,
                        lam, gain)


def _pair_blocks(lo, hi, n, width):
    return jnp.concatenate([lo, jnp.flip(hi, axis=1)], axis=1).reshape(n, width)


def diff_attention(p64, ppl, lam_params, g_sub, lambda_init, batch, seq):
    n = batch * seq
    qb = DIFF_Q_BLOCK
    chunk = 2 * qb
    nb = seq // qb
    half = nb // 2
    width = C_HEADS * 2 * C_HEAD_DIM
    out_spec = pl.BlockSpec((1, 1, qb, LANES), lambda b, h, i: (b, i, 0, h))
    out_shape = jax.ShapeDtypeStruct((batch, half, qb, width), BF16)
    lo, hi = pl.pallas_call(
        functools.partial(_diff_kernel, lambda_init=lambda_init),
        grid=(batch, C_HEADS, half),
        in_specs=[pl.BlockSpec((4, C_HEAD_DIM), lambda b, h, i: (0, 0)),
                  pl.BlockSpec((1, qb, LANES), lambda b, h, i: (R64_CQ + h, b * nb + i, 0)),
                  pl.BlockSpec((1, qb, LANES), lambda b, h, i: (R64_CQ + h, b * nb + nb - 1 - i, 0)),
                  pl.BlockSpec((1, seq, LANES), lambda b, h, i: (R64_CK + h, b, 0)),
                  pl.BlockSpec((1, seq, LANES), lambda b, h, i: (PL_CV + h, b, 0)),
                  pl.BlockSpec((1, LANES), lambda b, h, i: (0, 0))],
        out_specs=[out_spec, out_spec],
        out_shape=[out_shape, out_shape],
        scratch_shapes=[pltpu.VMEM((half + 1, 2 * qb, chunk), F32)],
        compiler_params=_cparams(("parallel", "parallel", "arbitrary")),
        name="diff_attention",
    )(lam_params, p64, p64, p64, ppl, g_sub)
    return _pair_blocks(lo, hi, n, width)


def _merge_kernel(oa_ref, ob_ref, oc_ref, wa_ref, wb_ref, wc_ref, ga_ref, gb_ref, gc_ref, o_ref):
    ya = _dot(oa_ref[...], wa_ref[...])
    yb = _dot(ob_ref[...], wb_ref[...])
    yc = _dot(oc_ref[...], wc_ref[...])
    for c in range(ga_ref.shape[0]):
        cols = slice(c * LANES, (c + 1) * LANES)
        o_ref[:, cols] = (ga_ref[c].astype(F32) * ya[:, cols] + gb_ref[c].astype(F32) * yb[:, cols]
                          + gc_ref[c].astype(F32) * yc[:, cols]).astype(o_ref.dtype)


def merge_branches(o_a, o_b, o_c, w_pa, w_pb, w_pc, gates, *, tm=512, tn=1024):
    n = o_a.shape[0]
    d = w_pa.shape[1]
    tpb = tn // LANES
    per_branch = d // tn
    act = lambda k: pl.BlockSpec((tm, k), lambda j, i: (i, 0))
    wsp = lambda k: pl.BlockSpec((k, tn), lambda j, i: (0, j))
    gsp = lambda br: pl.BlockSpec((tpb, tm, LANES), lambda j, i: (br * per_branch + j, i, 0))
    return pl.pallas_call(
        _merge_kernel,
        grid=(d // tn, n // tm),
        in_specs=[act(o_a.shape[1]), act(o_b.shape[1]), act(o_c.shape[1]),
                  wsp(w_pa.shape[0]), wsp(w_pb.shape[0]), wsp(w_pc.shape[0]),
                  gsp(0), gsp(1), gsp(2)],
        out_specs=pl.BlockSpec((tm, tn), lambda j, i: (i, j)),
        out_shape=jax.ShapeDtypeStruct((n, d), BF16),
        compiler_params=_cparams(("parallel", "arbitrary")),
        name="merge_branches",
    )(o_a, o_b, o_c, w_pa, w_pb, w_pc, gates, gates, gates)


def _matmul_residual_kernel(a_ref, w_ref, x_ref, o_ref):
    o_ref[...] = x_ref[...] + _dot(a_ref[...], w_ref[...])


def matmul_residual(a, w, x, *, tm=512, tn=2048):
    n, k = a.shape
    d = w.shape[1]
    return pl.pallas_call(
        _matmul_residual_kernel,
        grid=(n // tm, d // tn),
        in_specs=[pl.BlockSpec((tm, k), lambda i, j: (i, 0)),
                  pl.BlockSpec((k, tn), lambda i, j: (0, j)),
                  pl.BlockSpec((tm, tn), lambda i, j: (i, j))],
        out_specs=pl.BlockSpec((tm, tn), lambda i, j: (i, j)),
        out_shape=jax.ShapeDtypeStruct((n, d), F32),
        compiler_params=_cparams(("parallel", "arbitrary")),
        name="matmul_residual",
    )(a, w, x)


def _xattn_router_kernel(x_ref, gx_ref, wq_ref, kv_ref, wo_ref, gf_ref, wr_hi_ref, wr_lo_ref, br_ref,
                         x2_ref, route_ref, counts_ref):
    x = x_ref[...]
    ms = jnp.mean(x * x, axis=-1, keepdims=True)
    h = (x * lax.rsqrt(ms + NORM_EPS) * gx_ref[...]).astype(BF16)
    q = _dot(h, wq_ref[...]).astype(BF16)
    scale = X_HEAD_DIM ** -0.5
    outs = []
    for hd in range(X_HEADS):
        s = _dot_nt(q[:, hd * LANES:(hd + 1) * LANES], kv_ref[hd]) * scale
        m = jnp.max(s, axis=-1, keepdims=True)
        e = jnp.exp(s - m)
        l = jnp.sum(e, axis=-1, keepdims=True)
        outs.append((_dot_halves(e.astype(BF16), kv_ref[X_HEADS + hd]) / l).astype(BF16))
    x2 = x + _dot(jnp.concatenate(outs, axis=-1), wo_ref[...])
    x2_ref[...] = x2

    ms2 = jnp.mean(x2 * x2, axis=-1, keepdims=True)
    hn = x2 * lax.rsqrt(ms2 + NORM_EPS) * gf_ref[...]
    hn_hi = hn.astype(BF16)
    hn_lo = (hn - hn_hi.astype(F32)).astype(BF16)
    logits = (_dot_halves(hn_hi, wr_hi_ref[...]) + _dot_halves(hn_hi, wr_lo_ref[...])
              + _dot_halves(hn_lo, wr_hi_ref[...])
              + br_ref[...])

    lane = lax.broadcasted_iota(jnp.int32, logits.shape, 1).astype(F32)
    big = 1e9
    in_g = jnp.where(lane >= N_EXPERTS, jnp.where(lane < N_EXPERTS + N_GROUPS, 1.0, 0.0), 0.0) > 0.5
    lg = jnp.where(in_g, logits, NEG_INF)
    eg = jnp.where(in_g, jnp.exp(lg - jnp.max(lg, axis=-1, keepdims=True)), 0.0)
    pg = eg / jnp.sum(eg, axis=-1, keepdims=True)
    gp = jnp.max(pg, axis=-1, keepdims=True)
    gi = jnp.min(jnp.where(in_g, jnp.where(pg == gp, lane, big), big), axis=-1, keepdims=True) - N_EXPERTS
    first = EXPERTS_PER_GROUP * gi
    in_e = jnp.where(lane >= first, jnp.where(lane < first + EXPERTS_PER_GROUP, 1.0, 0.0), 0.0) > 0.5
    le = jnp.where(in_e, logits, NEG_INF)
    ee = jnp.where(in_e, jnp.exp(le - jnp.max(le, axis=-1, keepdims=True)), 0.0)
    pe = jnp.where(in_e, ee / jnp.sum(ee, axis=-1, keepdims=True), -1.0)
    p1 = jnp.max(pe, axis=-1, keepdims=True)
    i1 = jnp.min(jnp.where(pe == p1, lane, big), axis=-1, keepdims=True)
    pe2 = jnp.where(lane == i1, -1.0, pe)
    p2 = jnp.max(pe2, axis=-1, keepdims=True)
    i2 = jnp.min(jnp.where(pe2 == p2, jnp.where(in_e, jnp.where(lane == i1, big, lane), big), big),
                 axis=-1, keepdims=True)
    tot = p1 + p2
    route_ref[...] = jnp.where(lane == 0.0, i1, jnp.where(lane == 1.0, i2, jnp.where(
        lane == 2.0, gp * p1 / tot, jnp.where(lane == 3.0, gp * p2 / tot, 0.0))))

    first_step = (pl.program_id(0) == 0) & (pl.program_id(1) == 0)

    @pl.when(first_step)
    def _():
        counts_ref[...] = jnp.zeros(counts_ref.shape, F32)

    hits = jnp.where(lane == i1, 1.0, 0.0) + jnp.where(lane == i2, 1.0, 0.0)
    counts_ref[...] = counts_ref[...] + jnp.sum(hits, axis=0, keepdims=True)


def xattn_router(x, g_x, w_xq, kv, w_xo, g_ffn, wr_hi, wr_lo, b_r, batch, seq, n_mem, *, tm=512):
    n, d = x.shape
    per_b = seq // tm
    full = lambda shape: pl.BlockSpec(shape, lambda b, i: (0,) * len(shape))
    row = lambda width: pl.BlockSpec((tm, width), lambda b, i: (b * per_b + i, 0))
    return pl.pallas_call(
        _xattn_router_kernel,
        grid=(batch, per_b),
        in_specs=[row(d), full((1, d)), full((d, X_WIDTH)),
                  pl.BlockSpec((2 * X_HEADS, n_mem, LANES), lambda b, i: (0, b, 0)),
                  full((X_WIDTH, d)), full((1, d)), full((d, LANES)), full((d, LANES)), full((1, LANES))],
        out_specs=[row(d), row(LANES), full((1, LANES))],
        out_shape=[jax.ShapeDtypeStruct((n, d), F32), jax.ShapeDtypeStruct((n, LANES), F32),
                   jax.ShapeDtypeStruct((1, LANES), F32)],
        compiler_params=_cparams(("arbitrary", "arbitrary")),
        name="xattn_router",
    )(x, g_x, w_xq, kv, w_xo, g_ffn, wr_hi, wr_lo, b_r)


def _plan_kernel(route_ref, off_ref, dest_ref, run_scr):
    @pl.when(pl.program_id(0) == 0)
    def _():
        run_scr[...] = jnp.zeros(run_scr.shape, F32)

    tm = route_ref.shape[0]
    r = route_ref[...]
    lane = lax.broadcasted_iota(jnp.int32, r.shape, 1).astype(F32)
    a1 = jnp.where(lane == r[:, 0:1], 1.0, 0.0)
    a2 = jnp.where(lane == r[:, 1:2], 1.0, 0.0)
    hits = a1 + a2
    earlier = (lax.broadcasted_iota(jnp.int32, (tm, tm), 0) > lax.broadcasted_iota(jnp.int32, (tm, tm), 1))
    before = _dot_halves(jnp.where(earlier, 1.0, 0.0).astype(BF16), hits.astype(BF16))
    slot = off_ref[...] + run_scr[...] + before
    d1 = jnp.sum(a1 * slot, axis=-1, keepdims=True)
    d2 = jnp.sum(a2 * slot, axis=-1, keepdims=True)
    dest_ref[...] = jnp.where(lane == 0.0, d1, jnp.where(lane == 1.0, d2, 0.0)).astype(jnp.int32)
    run_scr[...] = run_scr[...] + jnp.sum(hits, axis=0, keepdims=True)


def dispatch_plan(route, offsets, *, tm=512):
    n = route.shape[0]
    return pl.pallas_call(
        _plan_kernel,
        grid=(n // tm,),
        in_specs=[pl.BlockSpec((tm, LANES), lambda i: (i, 0)), pl.BlockSpec((1, LANES), lambda i: (0, 0))],
        out_specs=pl.BlockSpec((tm, LANES), lambda i: (i, 0)),
        out_shape=jax.ShapeDtypeStruct((n, LANES), jnp.int32),
        scratch_shapes=[pltpu.VMEM((1, LANES), F32)],
        compiler_params=_cparams(("arbitrary",)),
        name="dispatch_plan",
    )(route, offsets)


def _row_copy(src, src_row, dst, dst_row, sem):
    return pltpu.make_async_copy(src.at[pl.ds(src_row, 1), :], dst.at[pl.ds(dst_row, 1), :], sem)


def _dispatch_kernel(zrow_ref, zflag_ref, d1_ref, d2_ref, x_ref, xs_ref, zero_scr, sems):
    tm = x_ref.shape[0]
    tile = zero_scr.shape[0]

    @pl.when(pl.program_id(0) == 0)
    def _():
        zero_scr[...] = jnp.zeros(zero_scr.shape, F32)
        fills = [pltpu.make_async_copy(zero_scr, xs_ref.at[pl.ds(pl.multiple_of(zrow_ref[e], 8), tile), :],
                                       sems.at[2]) for e in range(zrow_ref.shape[0])]
        for e, fill in enumerate(fills):
            pl.when(zflag_ref[e] > 0)(fill.start)
        for e, fill in enumerate(fills):
            pl.when(zflag_ref[e] > 0)(fill.wait)

    def issue(r, carry):
        _row_copy(x_ref, r, xs_ref, d1_ref[r], sems.at[0]).start()
        _row_copy(x_ref, r, xs_ref, d2_ref[r], sems.at[1]).start(priority=1)
        return carry

    lax.fori_loop(0, tm, issue, 0, unroll=8)
    pltpu.make_async_copy(x_ref, xs_ref.at[pl.ds(0, tm), :], sems.at[0]).wait()
    pltpu.make_async_copy(x_ref, xs_ref.at[pl.ds(0, tm), :], sems.at[1]).wait()


def dispatch_rows(x, d1, d2, zrows, zflags, rows_total, tile, *, tm=512):
    n, d = x.shape
    return pl.pallas_call(
        _dispatch_kernel,
        grid_spec=pltpu.PrefetchScalarGridSpec(
            num_scalar_prefetch=2,
            grid=(n // tm,),
            in_specs=[pl.BlockSpec((tm,), lambda i, z, f: (i,), memory_space=pltpu.SMEM),
                      pl.BlockSpec((tm,), lambda i, z, f: (i,), memory_space=pltpu.SMEM),
                      pl.BlockSpec((tm, d), lambda i, z, f: (i, 0))],
            out_specs=pl.BlockSpec(memory_space=pl.ANY),
            scratch_shapes=[pltpu.VMEM((tile, d), F32), pltpu.SemaphoreType.DMA((3,))]),
        out_shape=jax.ShapeDtypeStruct((rows_total, d), F32),
        compiler_params=_cparams(("arbitrary",)),
        name="dispatch_rows",
    )(zrows, zflags, d1, d2, x)


def _experts_kernel(te_ref, na_ref, xs_ref, g_ref, wg_ref, wu_ref, wd_ref, y_ref):
    active = pl.program_id(0) < na_ref[0]

    @pl.when(active)
    def _():
        x = xs_ref[...]
        ms = jnp.mean(x * x, axis=-1, keepdims=True)
        hn = (x * lax.rsqrt(ms + NORM_EPS) * g_ref[...]).astype(BF16)
        gate = _dot(hn, wg_ref[0, 0])
        up = _dot(hn, wu_ref[0, 0])
        hid = (gate * _sigmoid(gate) * up).astype(BF16)
        y_ref[...] = _dot(hid, wd_ref[0, 0])

    @pl.when(jnp.logical_not(active))
    def _():
        y_ref[...] = jnp.zeros(y_ref.shape, F32)


def expert_ffn(xs, g_ffn, w_gate, w_up, w_down, layer, tile_expert, n_active, tile):
    rows_total, d = xs.shape
    dff = w_gate.shape[-1]
    last = lambda i, te, na: jnp.minimum(i, na[0] - 1)
    return pl.pallas_call(
        _experts_kernel,
        grid_spec=pltpu.PrefetchScalarGridSpec(
            num_scalar_prefetch=2,
            grid=(rows_total // tile,),
            in_specs=[pl.BlockSpec((tile, d), lambda i, te, na: (last(i, te, na), 0)),
                      pl.BlockSpec((1, d), lambda i, te, na: (0, 0)),
                      pl.BlockSpec((1, 1, d, dff), lambda i, te, na: (layer, te[i], 0, 0)),
                      pl.BlockSpec((1, 1, d, dff), lambda i, te, na: (layer, te[i], 0, 0)),
                      pl.BlockSpec((1, 1, dff, d), lambda i, te, na: (layer, te[i], 0, 0))],
            out_specs=pl.BlockSpec((tile, d), lambda i, te, na: (i, 0))),
        out_shape=jax.ShapeDtypeStruct((rows_total, d), F32),
        compiler_params=_cparams(("arbitrary",)),
        name="expert_ffn",
    )(tile_expert, n_active, xs, g_ffn, w_gate, w_up, w_down)


def _combine_kernel(*refs, final_norm):
    if final_norm:
        d1_ref, d2_ref, x_ref, route_ref, y_ref, g_ref, o_ref, buf, sems = refs
    else:
        d1_ref, d2_ref, x_ref, route_ref, y_ref, o_ref, buf, sems = refs
    tm = x_ref.shape[0]

    def issue(r, carry):
        _row_copy(y_ref, d1_ref[r], buf.at[0], r, sems.at[0]).start()
        _row_copy(y_ref, d2_ref[r], buf.at[1], r, sems.at[1]).start(priority=1)
        return carry

    lax.fori_loop(0, tm, issue, 0, unroll=8)
    pltpu.make_async_copy(y_ref.at[pl.ds(0, tm), :], buf.at[0], sems.at[0]).wait()
    pltpu.make_async_copy(y_ref.at[pl.ds(0, tm), :], buf.at[1], sems.at[1]).wait()
    route = route_ref[...]
    o = x_ref[...] + (route[:, 2:3] * buf[0] + route[:, 3:4] * buf[1])
    if final_norm:
        ms = jnp.mean(o * o, axis=-1, keepdims=True)
        o = o * lax.rsqrt(ms + NORM_EPS) * g_ref[...]
    o_ref[...] = o


def combine_rows(x, route, y, d1, d2, final_gain=None, *, tm=512):
    n, d = x.shape
    in_specs = [pl.BlockSpec((tm,), lambda i: (i,), memory_space=pltpu.SMEM),
                pl.BlockSpec((tm,), lambda i: (i,), memory_space=pltpu.SMEM),
                pl.BlockSpec((tm, d), lambda i: (i, 0)),
                pl.BlockSpec((tm, LANES), lambda i: (i, 0)),
                pl.BlockSpec(memory_space=pl.ANY)]
    args = [d1, d2, x, route, y]
    if final_gain is not None:
        in_specs.append(pl.BlockSpec((1, d), lambda i: (0, 0)))
        args.append(final_gain)
    return pl.pallas_call(
        functools.partial(_combine_kernel, final_norm=final_gain is not None),
        grid=(n // tm,),
        in_specs=in_specs,
        out_specs=pl.BlockSpec((tm, d), lambda i: (i, 0)),
        out_shape=jax.ShapeDtypeStruct((n, d), F32),
        scratch_shapes=[pltpu.VMEM((2, tm, d), F32), pltpu.SemaphoreType.DMA((2,))],
        compiler_params=_cparams(("arbitrary",)),
        name="combine_rows",
    )(*args)


def routed_experts(x, route, counts, g_ffn, w_gate, w_up, w_down, layer, final_gain=None, *, tile=MOE_TILE):
    n, d = x.shape
    rows_total = 2 * n + N_EXPERTS * tile
    n_tiles = rows_total // tile
    cnt = counts[0, :N_EXPERTS].astype(jnp.int32)
    tiles = (cnt + tile - 1) // tile
    ends = jnp.cumsum(tiles)
    starts = ends - tiles
    n_active = ends[-1:]
    tile_ids = jnp.minimum(jnp.arange(n_tiles, dtype=jnp.int32), n_active[0] - 1)
    tile_expert = jnp.sum((tile_ids[:, None] >= ends[None, :]).astype(jnp.int32), axis=1)
    offsets = jnp.pad((starts * tile).astype(F32)[None, :], ((0, 0), (0, LANES - N_EXPERTS)))
    tail = n_active[0] + jnp.arange(N_EXPERTS, dtype=jnp.int32)
    zrows = (jnp.concatenate([jnp.maximum(ends - 1, starts), jnp.minimum(tail, n_tiles - 1)]) * tile).astype(jnp.int32)
    zflags = jnp.concatenate([tiles > 0, tail < n_tiles]).astype(jnp.int32)

    dest = dispatch_plan(route, offsets)
    d1, d2 = dest[:, 0], dest[:, 1]
    xs = dispatch_rows(x, d1, d2, zrows, zflags, rows_total, tile)
    y = expert_ffn(xs, g_ffn, w_gate, w_up, w_down, layer, tile_expert.astype(jnp.int32),
                   n_active.astype(jnp.int32), tile)
    return combine_rows(x, route, y, d1, d2, final_gain)


def _pad_cols(w, width):
    return jnp.pad(w, ((0, 0), (0, width - w.shape[1])))


def _regroup_kernel(w_ref, o64_ref, o128_ref, opl_ref, omg_ref):
    w = w_ref[0]
    cols = lambda lo, hi: w[:, lo:hi]
    o64_ref[...] = jnp.concatenate(
        [cols(_A_Q, _A_K), cols(_A_K, _A_V), cols(_C_Q, _C_K), cols(_C_K, _C_V)], axis=1).astype(BF16)
    o128_ref[...] = jnp.concatenate(
        [cols(_B_Q, _B_KC), cols(_B_KC, _B_VC), cols(_B_KS, _B_VS), cols(_B_KW, _B_VW)], axis=1).astype(BF16)
    per_group = 3 * B_HEADS // B_KV_GROUPS
    pad = jnp.zeros((w.shape[0], LANES - per_group), w.dtype)
    gate_tiles = []
    for g in range(B_KV_GROUPS):
        gate_tiles += [cols(_B_GATE + g * per_group, _B_GATE + (g + 1) * per_group), pad]
    opl_ref[...] = jnp.concatenate(
        [cols(_A_V, _B_Q), cols(_B_VC, _B_KS), cols(_B_VS, _B_KW), cols(_B_VW, _B_GATE)]
        + gate_tiles + [cols(_C_V, _MERGE)], axis=1).astype(BF16)
    omg_ref[...] = cols(_MERGE, _IN_END).astype(BF16)


def _input_weights(w_all, layer, *, tr=128):
    d = w_all.shape[1]
    widths = (R64_TILES * LANES, R128_TILES * LANES, PL_TILES * LANES, _IN_END - _MERGE)
    return pl.pallas_call(
        _regroup_kernel,
        grid=(d // tr,),
        in_specs=[pl.BlockSpec((1, tr, w_all.shape[2]), lambda i: (layer, i, 0))],
        out_specs=[pl.BlockSpec((tr, wd), lambda i: (i, 0)) for wd in widths],
        out_shape=[jax.ShapeDtypeStruct((d, wd), BF16) for wd in widths],
        compiler_params=_cparams(("parallel",)),
        name="regroup_w_in",
    )(w_all)


def kernel(x, mem, positions, g_mix, w_in, sinks_a, cmp_pos_k, cmp_pos_v, phi_k1, phi_k2, phi_v1, phi_v2,
           lq1, lk1, lq2, lk2, g_diff, w_pa, w_pb, w_pc, w_out, g_x, g_mem, w_xq, w_xkv, w_xo,
           g_ffn, w_group, b_group, w_expert, b_expert, w_gate, w_up, w_down, g_final):
    batch, seq, d = x.shape
    n_mem = mem.shape[1]
    n = batch * seq
    xf = x.reshape(n, d)
    memf = mem.reshape(batch * n_mem, d)
    pos_f = positions.reshape(n, 1).astype(F32)
    tab64 = rope_tables(pos_f, A_HEAD_DIM)
    tab128 = rope_tables(pos_f, B_HEAD_DIM)
    row = lambda v: v.reshape(1, -1)
    wg_all, wu_all, wd_all = w_gate.astype(BF16), w_up.astype(BF16), w_down.astype(BF16)

    for l in range(DEPTH):
        lambda_init = 0.8 - 0.6 * math.exp(-0.3 * l)
        w64, w128, wpl, wmg = _input_weights(w_in, l)
        g = row(g_mix[l])
        p64 = norm_proj(xf, g, w64, tm=512, tn=w64.shape[1], rope=64, tables=tab64)
        p128 = norm_proj(xf, g, w128, tm=512, tn=w128.shape[1], rope=128, tables=tab128)
        ppl = norm_proj(xf, g, wpl, tm=512, tn=wpl.shape[1])
        gates = norm_proj(xf, g, wmg, tm=1024, tn=1024, sigmoid_out=True)

        o_a = swa_sink_attention(p64, ppl, sinks_a[l], batch, seq)
        kcmp = compress_blocks(p128[R128_KC:R128_KC + B_KV_GROUPS], cmp_pos_k[l], phi_k1[l], phi_k2[l], batch, seq)
        vcmp = compress_blocks(ppl[PL_VC:PL_VC + B_KV_GROUPS], cmp_pos_v[l], phi_v1[l], phi_v2[l], batch, seq)
        o_b = nsa_attention(p128, ppl, kcmp, vcmp, batch, seq)
        lam_params = jnp.stack([lq1[l], lk1[l], lq2[l], lk2[l]])
        o_c = diff_attention(p64, ppl, lam_params, row(g_diff[l]), lambda_init, batch, seq)

        merged = merge_branches(o_a, o_b, o_c, w_pa[l].astype(BF16), w_pb[l].astype(BF16),
                                w_pc[l].astype(BF16), gates)
        xf = matmul_residual(merged, w_out[l].astype(BF16), xf)

        kv = norm_proj(memf, row(g_mem[l]), w_xkv[l].astype(BF16), tm=512, tn=2 * X_WIDTH)
        w_r = _pad_cols(jnp.concatenate([w_expert[l], w_group[l]], axis=1), LANES)
        wr_hi = w_r.astype(BF16)
        wr_lo = (w_r - wr_hi.astype(F32)).astype(BF16)
        b_r = _pad_cols(jnp.concatenate([b_expert[l], b_group[l]])[None, :], LANES)
        xf, route, counts = xattn_router(xf, row(g_x[l]), w_xq[l].astype(BF16), kv, w_xo[l].astype(BF16),
                                         row(g_ffn[l]), wr_hi, wr_lo, b_r, batch, seq, n_mem)
        xf = routed_experts(xf, route, counts, row(g_ffn[l]), wg_all, wu_all, wd_all, l,
                            final_gain=row(g_final) if l == DEPTH - 1 else None)

    return xf.reshape(batch, seq, d)
```

```python
import functools
import math

import jax
import jax.numpy as jnp
from jax import lax
from jax.experimental import pallas as pl
from jax.experimental.pallas import tpu as pltpu

D_MODEL = 2048
DEPTH = 2
ROPE_THETA = 10000.0
NORM_EPS = 1e-6
BLOCK_Q = 128
NEG_INF = -1e30
FORCE_SCORE = 1e6
SWA_Q_BLOCK = 128
NSA_Q_BLOCK = 512
NSA_SLC_CHUNK = 512
DIFF_Q_BLOCK = 256
MOE_TILE = 256

A_HEADS, A_KV_HEADS, A_HEAD_DIM, A_WINDOW = 8, 2, 64, 128
B_HEADS, B_KV_GROUPS, B_HEAD_DIM = 8, 2, 128
CMP_LEN, CMP_STRIDE, SLC_LEN, SLC_TOPN, B_WINDOW = 32, 16, 64, 8, 512
C_HEADS, C_HEAD_DIM = 4, 64
X_HEADS, X_HEAD_DIM = 4, 128
X_WIDTH = X_HEADS * X_HEAD_DIM
N_GROUPS, EXPERTS_PER_GROUP = 4, 4
N_EXPERTS = N_GROUPS * EXPERTS_PER_GROUP
D_FF_EXPERT = 1024

LANES = 128
VMEM_LIMIT = 52 * 1024 * 1024

_A_Q, _A_K, _A_V = 0, 512, 640
_B_Q, _B_KC, _B_VC, _B_KS, _B_VS, _B_KW, _B_VW, _B_GATE = 768, 1792, 2048, 2304, 2560, 2816, 3072, 3328
_C_Q, _C_K, _C_V, _MERGE, _IN_END = 3352, 3864, 4376, 4888, 11032

R64_AQ, R64_AK, R64_CQ, R64_CK, R64_TILES = 0, 4, 5, 9, 13
R128_BQ, R128_KC, R128_KS, R128_KW, R128_TILES = 0, 8, 10, 12, 14
PL_AV, PL_VC, PL_VS, PL_VW, PL_GATE, PL_CV, PL_TILES = 0, 1, 3, 5, 7, 9, 13

BF16 = jnp.bfloat16
F32 = jnp.float32


def _cparams(sem):
    return pltpu.CompilerParams(dimension_semantics=sem, vmem_limit_bytes=VMEM_LIMIT)


def _dot(a, b):
    return jnp.dot(a, b, preferred_element_type=F32)


def _dot_halves(a, b):
    half = a.shape[0] // 2
    return jnp.concatenate([_dot(a[0:half], b), _dot(a[half:], b)], axis=0)


def _dot_nt(a, b):
    return lax.dot_general(a, b, (((1,), (1,)), ((), ())), preferred_element_type=F32)


def _sigmoid(x):
    return 0.5 * jnp.tanh(0.5 * x) + 0.5


def _rope_table_kernel(pos_ref, inv_ref, sign_ref, cos_ref, sin_ref):
    ang = pos_ref[...] * inv_ref[...]
    cos_ref[...] = jnp.cos(ang)
    sin_ref[...] = jnp.sin(ang) * sign_ref[...]


def rope_tables(pos_f, head_dim):
    n = pos_f.shape[0]
    half = head_dim // 2
    lane = jnp.arange(LANES)
    inv = jnp.power(ROPE_THETA, -(2.0 * (lane % half).astype(F32)) / head_dim)[None, :]
    sign = jnp.where((lane % head_dim) < half, -1.0, 1.0).astype(F32)[None, :]
    tm = 2048
    return pl.pallas_call(
        _rope_table_kernel,
        grid=(n // tm,),
        in_specs=[pl.BlockSpec((tm, 1), lambda i: (i, 0)),
                  pl.BlockSpec((1, LANES), lambda i: (0, 0)),
                  pl.BlockSpec((1, LANES), lambda i: (0, 0))],
        out_specs=[pl.BlockSpec((tm, LANES), lambda i: (i, 0)),
                   pl.BlockSpec((tm, LANES), lambda i: (i, 0))],
        out_shape=[jax.ShapeDtypeStruct((n, LANES), F32)] * 2,
        compiler_params=_cparams(("parallel",)),
        name="rope_tables",
    )(pos_f, inv, sign)


def _norm_proj_kernel(*refs, rope, sigmoid_out):
    if rope:
        x_ref, g_ref, w_ref, cos_ref, sin_ref, o_ref, h_scr = refs
    else:
        x_ref, g_ref, w_ref, o_ref, h_scr = refs

    @pl.when(pl.program_id(1) == 0)
    def _():
        x = x_ref[...]
        ms = jnp.mean(x * x, axis=-1, keepdims=True)
        h_scr[...] = (x * lax.rsqrt(ms + NORM_EPS) * g_ref[...]).astype(BF16)

    acc = _dot(h_scr[...], w_ref[...])
    if rope:
        cos = cos_ref[...]
        sin = sin_ref[...]
        if rope == 64:
            first_half = (lax.broadcasted_iota(jnp.int32, cos.shape, 1) & 63) < 32
    for c in range(o_ref.shape[0]):
        a = acc[:, c * LANES:(c + 1) * LANES]
        if rope == 128:
            a = a * cos + pltpu.roll(a, 64, 1) * sin
        elif rope == 64:
            partner = jnp.where(first_half, pltpu.roll(a, 96, 1), pltpu.roll(a, 32, 1))
            a = a * cos + partner * sin
        if sigmoid_out:
            a = _sigmoid(a)
        o_ref[c] = a.astype(o_ref.dtype)


def norm_proj(x, g, w, *, tm, tn, rope=0, tables=None, sigmoid_out=False):
    n, d = x.shape
    tiles = w.shape[1] // LANES
    tpb = tn // LANES
    in_specs = [pl.BlockSpec((tm, d), lambda i, j: (i, 0)),
                pl.BlockSpec((1, d), lambda i, j: (0, 0)),
                pl.BlockSpec((d, tn), lambda i, j: (0, j))]
    args = [x, g, w]
    if rope:
        in_specs += [pl.BlockSpec((tm, LANES), lambda i, j: (i, 0))] * 2
        args += list(tables)
    return pl.pallas_call(
        functools.partial(_norm_proj_kernel, rope=rope, sigmoid_out=sigmoid_out),
        grid=(n // tm, tiles // tpb),
        in_specs=in_specs,
        out_specs=pl.BlockSpec((tpb, tm, LANES), lambda i, j: (j, i, 0)),
        out_shape=jax.ShapeDtypeStruct((tiles, n, LANES), BF16),
        scratch_shapes=[pltpu.VMEM((tm, d), BF16)],
        compiler_params=_cparams(("parallel", "arbitrary")),
        name="norm_proj",
    )(*args)


def _swa_kernel(sink_ref, q_ref, k_ref, v_ref, o_ref):
    n = pl.program_id(1)
    qb = q_ref.shape[1]
    span = A_WINDOW + qb
    start = pl.multiple_of(jnp.maximum(n * qb - A_WINDOW, 0), A_WINDOW)
    kk = k_ref[0, pl.ds(start, span), :]
    vv = v_ref[0, pl.ds(start, span), :]
    qpos = n * qb + lax.broadcasted_iota(jnp.int32, (qb, span), 0)
    kpos = start + lax.broadcasted_iota(jnp.int32, (qb, span), 1)
    rel = qpos - kpos
    scale = A_HEAD_DIM ** -0.5
    heads_per_kv = A_HEADS // A_KV_HEADS
    for t in range(A_HEADS // 2):
        qt = q_ref[t]
        outs = []
        for hh in range(2):
            h = 2 * t + hh
            g = h // heads_per_kv
            qh = qt[:, hh * 64:(hh + 1) * 64]
            kh = kk[:, g * 64:(g + 1) * 64]
            vh = vv[:, g * 64:(g + 1) * 64]
            s = _dot_nt(qh, kh) * scale
            s = jnp.where(rel >= 0, jnp.where(rel < A_WINDOW, s, NEG_INF), NEG_INF)
            sk = sink_ref[h]
            m = jnp.maximum(jnp.max(s, axis=-1, keepdims=True), sk)
            e = jnp.exp(s - m)
            denom = jnp.sum(e, axis=-1, keepdims=True) + jnp.exp(sk - m)
            outs.append(_dot(e.astype(BF16), vh) / denom)
        o_ref[:, t * LANES:(t + 1) * LANES] = jnp.concatenate(outs, axis=-1).astype(o_ref.dtype)


def swa_sink_attention(p64, ppl, sinks, batch, seq):
    n = batch * seq
    qb = SWA_Q_BLOCK
    nb = seq // qb
    return pl.pallas_call(
        _swa_kernel,
        grid=(batch, nb),
        in_specs=[pl.BlockSpec(memory_space=pltpu.SMEM),
                  pl.BlockSpec((A_HEADS // 2, qb, LANES), lambda b, i: (R64_AQ // 4, b * nb + i, 0)),
                  pl.BlockSpec((1, seq, LANES), lambda b, i: (R64_AK, b, 0)),
                  pl.BlockSpec((1, seq, LANES), lambda b, i: (PL_AV, b, 0))],
        out_specs=pl.BlockSpec((qb, A_HEADS * A_HEAD_DIM), lambda b, i: (b * nb + i, 0)),
        out_shape=jax.ShapeDtypeStruct((n, A_HEADS * A_HEAD_DIM), BF16),
        compiler_params=_cparams(("parallel", "arbitrary")),
        name="swa_sink",
    )(sinks, p64, p64, ppl)


def _compress_kernel(t_ref, pe_ref, w1_ref, w2_ref, o_ref):
    t = t_ref[0, 0]
    half = t.shape[1]
    lo = _dot(t, w1_ref[0:half, :])
    hi = _dot(t, w1_ref[half:2 * half, :])
    rows = t.shape[0]
    hi = pltpu.roll(hi, rows - 1, 0)
    pe = jnp.broadcast_to(pe_ref[...], (8, 2 * half))
    pc = _dot(pe, w1_ref[...])[0:1, :]
    hid = lo + hi + pc
    hid = hid * _sigmoid(hid)
    o_ref[0, 0] = _dot(hid.astype(BF16), w2_ref[...]).astype(o_ref.dtype)


def compress_blocks(tiles, pos_emb, w1, w2, batch, seq):
    g = tiles.shape[0]
    rows = seq // CMP_STRIDE
    t2 = tiles.reshape(g, batch, rows, CMP_STRIDE * LANES)
    pe = pos_emb.reshape(1, CMP_LEN * LANES).astype(BF16)
    return pl.pallas_call(
        _compress_kernel,
        grid=(g, batch),
        in_specs=[pl.BlockSpec((1, 1, rows, CMP_STRIDE * LANES), lambda i, b: (i, b, 0, 0)),
                  pl.BlockSpec((1, CMP_LEN * LANES), lambda i, b: (0, 0)),
                  pl.BlockSpec((CMP_LEN * LANES, LANES), lambda i, b: (0, 0)),
                  pl.BlockSpec((LANES, LANES), lambda i, b: (0, 0))],
        out_specs=pl.BlockSpec((1, 1, rows, LANES), lambda i, b: (i, b, 0, 0)),
        out_shape=jax.ShapeDtypeStruct((g, batch, rows, LANES), BF16),
        compiler_params=_cparams(("parallel", "parallel")),
        name="compress_blocks",
    )(t2, pe, w1.astype(BF16), w2.astype(BF16))


def _nsa_kernel(q_ref, kc_ref, vc_ref, ks_ref, vs_ref, kw_ref, vw_ref, gate_ref, ovlt_ref, o_ref,
                s_scr, mx_scr, ls_scr, acc_scr, *, seq):
    n = pl.program_id(2)
    r = B_HEADS // B_KV_GROUPS
    qb = q_ref.shape[1]
    rq = r * qb
    scale = B_HEAD_DIM ** -0.5
    q4 = q_ref[...].reshape(rq, LANES)
    tpos = n * qb + lax.broadcasted_iota(jnp.int32, (qb, LANES), 0)
    lane = lax.broadcasted_iota(jnp.int32, (qb, LANES), 1)

    nc = seq // CMP_STRIDE - 1
    s = (_dot_nt(q4, kc_ref[0, 0]) * scale).reshape(r, qb, LANES)
    ok = (tpos >= lane * CMP_STRIDE + (CMP_LEN - 1)) & (lane < nc)
    s = jnp.where(ok[None], s, NEG_INF)
    m = jnp.max(s, axis=-1, keepdims=True)
    e = jnp.where(ok[None], jnp.exp(s - m), 0.0)
    l = jnp.sum(e, axis=-1, keepdims=True)
    p = e / jnp.where(l > 0.0, l, 1.0)
    o_cmp = _dot_halves(p.reshape(rq, LANES).astype(BF16), vc_ref[0, 0])

    psum = p[0] + p[1] + p[2] + p[3]
    p_hi = psum.astype(BF16)
    p_mid = (psum - p_hi.astype(F32)).astype(BF16)
    p_lo = (psum - p_hi.astype(F32) - p_mid.astype(F32)).astype(BF16)
    ns = seq // SLC_LEN
    ovl_t = ovlt_ref[...]
    imp_t = (_dot_nt(ovl_t, p_hi) + _dot_nt(ovl_t, p_mid) + _dot_nt(ovl_t, p_lo))[0:ns]
    blk = lax.broadcasted_iota(jnp.int32, (ns, qb), 0)
    tq = n * qb + lax.broadcasted_iota(jnp.int32, (ns, qb), 1)
    cur = tq >> 6
    forced = (blk == 0) | (blk == cur) | (blk == cur - 1)
    future = blk * SLC_LEN > tq
    key = jnp.where(future, -1.0, jnp.where(forced, FORCE_SCORE, imp_t))
    rank = jnp.zeros((ns, qb), F32)
    for i in range(ns):
        ki = key[i:i + 1, :]
        rank = rank + jnp.where(blk > i, jnp.where(ki >= key, 1.0, 0.0), jnp.where(ki > key, 1.0, 0.0))
    sel_t = jnp.where(rank < float(min(SLC_TOPN, ns)), 1.0, 0.0)
    sel_t = jnp.concatenate([sel_t, jnp.zeros((LANES - ns, qb), F32)], axis=0)
    selm = sel_t.T.astype(BF16)

    chunk = s_scr.shape[2]
    halves = chunk // LANES
    mx_scr[...] = jnp.full(mx_scr.shape, NEG_INF, F32)
    ls_scr[...] = jnp.zeros(ls_scr.shape, F32)
    acc_scr[...] = jnp.zeros(acc_scr.shape, F32)
    blk_of_key = lax.broadcasted_iota(jnp.int32, (LANES, chunk), 1) >> 6
    blk_row = lax.broadcasted_iota(jnp.int32, (LANES, chunk), 0)
    kcol = lax.broadcasted_iota(jnp.int32, (qb, chunk), 1)
    trow = n * qb + lax.broadcasted_iota(jnp.int32, (qb, chunk), 0)
    trips = (n * qb + qb + chunk - 1) // chunk

    def slc_scores(c, carry):
        base = pl.multiple_of(c * chunk, chunk)
        kk = ks_ref[0, pl.ds(base, chunk), :]
        expand = jnp.where(blk_row == blk_of_key + c * (chunk // SLC_LEN), 1.0, 0.0).astype(BF16)
        keep = jnp.where((base + kcol) <= trow, _dot(selm, expand), 0.0) > 0.5
        sc = (_dot_nt(q4, kk) * scale).reshape(r, qb, chunk)
        sc = jnp.where(keep[None], sc, NEG_INF).reshape(rq, chunk)
        s_scr[c] = sc
        mx = mx_scr[...]
        for j in range(halves):
            mx = jnp.maximum(mx, sc[:, j * LANES:(j + 1) * LANES])
        mx_scr[...] = mx
        return carry

    lax.fori_loop(0, trips, slc_scores, 0)
    mx_scr[...] = jnp.broadcast_to(jnp.max(mx_scr[...], axis=-1, keepdims=True), mx_scr.shape)

    def slc_values(c, carry):
        base = pl.multiple_of(c * chunk, chunk)
        vv = vs_ref[0, pl.ds(base, chunk), :]
        sc = s_scr[c]
        mb = mx_scr[...]
        es = [jnp.exp(sc[:, j * LANES:(j + 1) * LANES] - mb) for j in range(halves)]
        ls_scr[...] = ls_scr[...] + functools.reduce(lambda a, b: a + b, es)
        acc_scr[...] = acc_scr[...] + _dot_halves(jnp.concatenate(es, axis=-1).astype(BF16), vv)
        return carry

    lax.fori_loop(0, trips, slc_values, 0)
    o_slc = acc_scr[...] / jnp.sum(ls_scr[...], axis=-1, keepdims=True)

    span = B_WINDOW + qb
    start = pl.multiple_of(jnp.maximum(n - B_WINDOW // qb, 0) * qb, qb)
    rel = (n * qb + lax.broadcasted_iota(jnp.int32, (qb, span), 0)
           - start - lax.broadcasted_iota(jnp.int32, (qb, span), 1))
    sw = (_dot_nt(q4, kw_ref[0, pl.ds(start, span), :]) * scale).reshape(r, qb, span)
    sw = jnp.where((rel >= 0)[None], jnp.where((rel < B_WINDOW)[None], sw, NEG_INF), NEG_INF)
    mw = jnp.max(sw, axis=-1, keepdims=True)
    ew = jnp.exp(sw - mw)
    lw = jnp.sum(ew, axis=-1, keepdims=True).reshape(rq, 1)
    o_win = _dot_halves(ew.reshape(rq, span).astype(BF16), vw_ref[0, pl.ds(start, span), :]) / lw

    gates = _sigmoid(gate_ref[0].astype(F32))
    for hh in range(r):
        rows = slice(hh * qb, (hh + 1) * qb)
        o = (gates[:, 3 * hh:3 * hh + 1] * o_cmp[rows]
             + gates[:, 3 * hh + 1:3 * hh + 2] * o_slc[rows]
             + gates[:, 3 * hh + 2:3 * hh + 3] * o_win[rows])
        o_ref[:, hh * LANES:(hh + 1) * LANES] = o.astype(o_ref.dtype)


def _overlap_matrix_t(seq):
    nc = seq // CMP_STRIDE - 1
    ns = seq // SLC_LEN
    j = jnp.arange(LANES)[:, None]
    c = jnp.arange(LANES)[None, :]
    c_start = c * CMP_STRIDE
    hit = (c_start < (j + 1) * SLC_LEN) & (c_start + CMP_LEN > j * SLC_LEN) & (c < nc) & (j < ns)
    return hit.astype(BF16)


def nsa_attention(p128, ppl, kcmp, vcmp, batch, seq):
    n = batch * seq
    qb = NSA_Q_BLOCK
    nb = seq // qb
    r = B_HEADS // B_KV_GROUPS
    rows = seq // CMP_STRIDE
    tok = lambda b, g, i: b * nb + i
    return pl.pallas_call(
        functools.partial(_nsa_kernel, seq=seq),
        grid=(batch, B_KV_GROUPS, nb),
        in_specs=[pl.BlockSpec((r, qb, LANES), lambda b, g, i: (R128_BQ // r + g, tok(b, g, i), 0)),
                  pl.BlockSpec((1, 1, rows, LANES), lambda b, g, i: (g, b, 0, 0)),
                  pl.BlockSpec((1, 1, rows, LANES), lambda b, g, i: (g, b, 0, 0)),
                  pl.BlockSpec((1, seq, LANES), lambda b, g, i: (R128_KS + g, b, 0)),
                  pl.BlockSpec((1, seq, LANES), lambda b, g, i: (PL_VS + g, b, 0)),
                  pl.BlockSpec((1, seq, LANES), lambda b, g, i: (R128_KW + g, b, 0)),
                  pl.BlockSpec((1, seq, LANES), lambda b, g, i: (PL_VW + g, b, 0)),
                  pl.BlockSpec((1, qb, LANES), lambda b, g, i: (PL_GATE + g, tok(b, g, i), 0)),
                  pl.BlockSpec((LANES, LANES), lambda b, g, i: (0, 0))],
        out_specs=pl.BlockSpec((qb, r * LANES), lambda b, g, i: (tok(b, g, i), g)),
        out_shape=jax.ShapeDtypeStruct((n, B_HEADS * B_HEAD_DIM), BF16),
        scratch_shapes=[pltpu.VMEM((seq // NSA_SLC_CHUNK, r * qb, NSA_SLC_CHUNK), F32),
                        pltpu.VMEM((r * qb, LANES), F32), pltpu.VMEM((r * qb, LANES), F32),
                        pltpu.VMEM((r * qb, LANES), F32)],
        compiler_params=_cparams(("parallel", "parallel", "arbitrary")),
        name="nsa_attention",
    )(p128, kcmp, vcmp, p128, ppl, p128, ppl, ppl, _overlap_matrix_t(seq))


def _diff_block(q_ref, k_ref, v_ref, o_ref, s_scr, slot0, first_query, n_chunks, lam, gain):
    qb = q_ref.shape[1]
    _, rows, chunk = s_scr.shape
    groups = chunk // LANES
    q = q_ref[0] * (C_HEAD_DIM ** -0.5)
    lane = lax.broadcasted_iota(jnp.int32, (qb, LANES), 1)
    zero = jnp.zeros_like(q)
    q2 = jnp.concatenate([jnp.where(lane < C_HEAD_DIM, q, zero), jnp.where(lane >= C_HEAD_DIM, q, zero)], axis=0)

    mx = jnp.full((rows, LANES), NEG_INF, F32)
    for c in range(n_chunks):
        sc = _dot_nt(q2, k_ref[0, c * chunk:(c + 1) * chunk, :])
        if c == n_chunks - 1:
            ahead = (lax.broadcasted_iota(jnp.int32, (rows, chunk), 1)
                     - (lax.broadcasted_iota(jnp.int32, (rows, chunk), 0) & (qb - 1)))
            sc = jnp.where(ahead <= first_query - c * chunk, sc, NEG_INF)
        s_scr[slot0 + c] = sc
        for j in range(groups):
            mx = jnp.maximum(mx, sc[:, j * LANES:(j + 1) * LANES])
    mb = jnp.broadcast_to(jnp.max(mx, axis=-1, keepdims=True), (rows, LANES))

    ls = jnp.zeros((rows, LANES), F32)
    acc = jnp.zeros((rows, LANES), F32)
    for c in range(n_chunks):
        sc = s_scr[slot0 + c]
        es = [jnp.exp(sc[:, j * LANES:(j + 1) * LANES] - mb) for j in range(groups)]
        ls = ls + functools.reduce(lambda a, b: a + b, es)
        acc = acc + _dot_halves(jnp.concatenate(es, axis=-1).astype(BF16),
                                v_ref[0, c * chunk:(c + 1) * chunk, :])
    a = acc / jnp.sum(ls, axis=-1, keepdims=True)
    o = a[0:qb] - lam * a[qb:rows]
    ms = jnp.mean(o * o, axis=-1, keepdims=True)
    o_ref[0, 0] = (o * lax.rsqrt(ms + NORM_EPS) * gain).astype(o_ref.dtype)


def _diff_kernel(lam_ref, qlo_ref, qhi_ref, k_ref, v_ref, g_ref, olo_ref, ohi_ref, s_scr, *, lambda_init):
    i = pl.program_id(2)
    half = pl.num_programs(2)
    qb = qlo_ref.shape[1]
    total, _, chunk = s_scr.shape
    lp = lam_ref[...]
    lam = (jnp.exp(jnp.sum(lp[0:1] * lp[1:2], axis=-1, keepdims=True))
           - jnp.exp(jnp.sum(lp[2:3] * lp[3:4], axis=-1, keepdims=True)) + lambda_init)
    gain = g_ref[...] * (1.0 - lambda_init)
    chunks_lo = lax.shift_right_logical(i * qb + qb + chunk - 1, chunk.bit_length() - 1)
    for n_lo in range(1, (half * qb + chunk - 1) // chunk + 1):
        @pl.when(chunks_lo == n_lo)
        def _():
            _diff_block(qlo_ref, k_ref, v_ref, olo_ref, s_scr, 0, i * qb, n_lo, lam, gain)
            _diff_block(qhi_ref, k_ref, v_ref, ohi_ref, s_scr, n_lo, (2 * half - 1 - i) * qb, total - n_lo,
                        lam, gain)


def _pair_blocks(lo, hi, n, width):
    return jnp.concatenate([lo, jnp.flip(hi, axis=1)], axis=1).reshape(n, width)


def diff_attention(p64, ppl, lam_params, g_sub, lambda_init, batch, seq):
    n = batch * seq
    qb = DIFF_Q_BLOCK
    chunk = 2 * qb
    nb = seq // qb
    half = nb // 2
    width = C_HEADS * 2 * C_HEAD_DIM
    out_spec = pl.BlockSpec((1, 1, qb, LANES), lambda b, h, i: (b, i, 0, h))
    out_shape = jax.ShapeDtypeStruct((batch, half, qb, width), BF16)
    lo, hi = pl.pallas_call(
        functools.partial(_diff_kernel, lambda_init=lambda_init),
        grid=(batch, C_HEADS, half),
        in_specs=[pl.BlockSpec((4, C_HEAD_DIM), lambda b, h, i: (0, 0)),
                  pl.BlockSpec((1, qb, LANES), lambda b, h, i: (R64_CQ + h, b * nb + i, 0)),
                  pl.BlockSpec((1, qb, LANES), lambda b, h, i: (R64_CQ + h, b * nb + nb - 1 - i, 0)),
                  pl.BlockSpec((1, seq, LANES), lambda b, h, i: (R64_CK + h, b, 0)),
                  pl.BlockSpec((1, seq, LANES), lambda b, h, i: (PL_CV + h, b, 0)),
                  pl.BlockSpec((1, LANES), lambda b, h, i: (0, 0))],
        out_specs=[out_spec, out_spec],
        out_shape=[out_shape, out_shape],
        scratch_shapes=[pltpu.VMEM((half + 1, 2 * qb, chunk), F32)],
        compiler_params=_cparams(("parallel", "parallel", "arbitrary")),
        name="diff_attention",
    )(lam_params, p64, p64, p64, ppl, g_sub)
    return _pair_blocks(lo, hi, n, width)


def _merge_kernel(oa_ref, ob_ref, oc_ref, wa_ref, wb_ref, wc_ref, ga_ref, gb_ref, gc_ref, o_ref):
    ya = _dot(oa_ref[...], wa_ref[...])
    yb = _dot(ob_ref[...], wb_ref[...])
    yc = _dot(oc_ref[...], wc_ref[...])
    for c in range(ga_ref.shape[0]):
        cols = slice(c * LANES, (c + 1) * LANES)
        o_ref[:, cols] = (ga_ref[c].astype(F32) * ya[:, cols] + gb_ref[c].astype(F32) * yb[:, cols]
                          + gc_ref[c].astype(F32) * yc[:, cols]).astype(o_ref.dtype)


def merge_branches(o_a, o_b, o_c, w_pa, w_pb, w_pc, gates, *, tm=512, tn=1024):
    n = o_a.shape[0]
    d = w_pa.shape[1]
    tpb = tn // LANES
    per_branch = d // tn
    act = lambda k: pl.BlockSpec((tm, k), lambda j, i: (i, 0))
    wsp = lambda k: pl.BlockSpec((k, tn), lambda j, i: (0, j))
    gsp = lambda br: pl.BlockSpec((tpb, tm, LANES), lambda j, i: (br * per_branch + j, i, 0))
    return pl.pallas_call(
        _merge_kernel,
        grid=(d // tn, n // tm),
        in_specs=[act(o_a.shape[1]), act(o_b.shape[1]), act(o_c.shape[1]),
                  wsp(w_pa.shape[0]), wsp(w_pb.shape[0]), wsp(w_pc.shape[0]),
                  gsp(0), gsp(1), gsp(2)],
        out_specs=pl.BlockSpec((tm, tn), lambda j, i: (i, j)),
        out_shape=jax.ShapeDtypeStruct((n, d), BF16),
        compiler_params=_cparams(("parallel", "arbitrary")),
        name="merge_branches",
    )(o_a, o_b, o_c, w_pa, w_pb, w_pc, gates, gates, gates)


def _matmul_residual_kernel(a_ref, w_ref, x_ref, o_ref):
    o_ref[...] = x_ref[...] + _dot(a_ref[...], w_ref[...])


def matmul_residual(a, w, x, *, tm=512, tn=2048):
    n, k = a.shape
    d = w.shape[1]
    return pl.pallas_call(
        _matmul_residual_kernel,
        grid=(n // tm, d // tn),
        in_specs=[pl.BlockSpec((tm, k), lambda i, j: (i, 0)),
                  pl.BlockSpec((k, tn), lambda i, j: (0, j)),
                  pl.BlockSpec((tm, tn), lambda i, j: (i, j))],
        out_specs=pl.BlockSpec((tm, tn), lambda i, j: (i, j)),
        out_shape=jax.ShapeDtypeStruct((n, d), F32),
        compiler_params=_cparams(("parallel", "arbitrary")),
        name="matmul_residual",
    )(a, w, x)


def _xattn_router_kernel(x_ref, gx_ref, wq_ref, kv_ref, wo_ref, gf_ref, wr_ref, br_ref,
                         x2_ref, route_ref, counts_ref):
    x = x_ref[...]
    ms = jnp.mean(x * x, axis=-1, keepdims=True)
    h = (x * lax.rsqrt(ms + NORM_EPS) * gx_ref[...]).astype(BF16)
    q = _dot(h, wq_ref[...]).astype(BF16)
    scale = X_HEAD_DIM ** -0.5
    outs = []
    for hd in range(X_HEADS):
        s = _dot_nt(q[:, hd * LANES:(hd + 1) * LANES], kv_ref[hd]) * scale
        m = jnp.max(s, axis=-1, keepdims=True)
        e = jnp.exp(s - m)
        l = jnp.sum(e, axis=-1, keepdims=True)
        outs.append((_dot_halves(e.astype(BF16), kv_ref[X_HEADS + hd]) / l).astype(BF16))
    x2 = x + _dot(jnp.concatenate(outs, axis=-1), wo_ref[...])
    x2_ref[...] = x2

    ms2 = jnp.mean(x2 * x2, axis=-1, keepdims=True)
    hn = x2 * lax.rsqrt(ms2 + NORM_EPS) * gf_ref[...]
    logits = _dot_halves(hn.astype(BF16), wr_ref[...]) + br_ref[...]

    lane = lax.broadcasted_iota(jnp.int32, logits.shape, 1).astype(F32)
    big = 1e9
    in_g = jnp.where(lane >= N_EXPERTS, jnp.where(lane < N_EXPERTS + N_GROUPS, 1.0, 0.0), 0.0) > 0.5
    lg = jnp.where(in_g, logits, NEG_INF)
    eg = jnp.where(in_g, jnp.exp(lg - jnp.max(lg, axis=-1, keepdims=True)), 0.0)
    pg = eg / jnp.sum(eg, axis=-1, keepdims=True)
    gp = jnp.max(pg, axis=-1, keepdims=True)
    gi = jnp.min(jnp.where(in_g, jnp.where(pg == gp, lane, big), big), axis=-1, keepdims=True) - N_EXPERTS
    first = EXPERTS_PER_GROUP * gi
    in_e = jnp.where(lane >= first, jnp.where(lane < first + EXPERTS_PER_GROUP, 1.0, 0.0), 0.0) > 0.5
    le = jnp.where(in_e, logits, NEG_INF)
    ee = jnp.where(in_e, jnp.exp(le - jnp.max(le, axis=-1, keepdims=True)), 0.0)
    pe = jnp.where(in_e, ee / jnp.sum(ee, axis=-1, keepdims=True), -1.0)
    p1 = jnp.max(pe, axis=-1, keepdims=True)
    i1 = jnp.min(jnp.where(pe == p1, lane, big), axis=-1, keepdims=True)
    pe2 = jnp.where(lane == i1, -1.0, pe)
    p2 = jnp.max(pe2, axis=-1, keepdims=True)
    i2 = jnp.min(jnp.where(pe2 == p2, jnp.where(in_e, jnp.where(lane == i1, big, lane), big), big),
                 axis=-1, keepdims=True)
    tot = p1 + p2
    route_ref[...] = jnp.where(lane == 0.0, i1, jnp.where(lane == 1.0, i2, jnp.where(
        lane == 2.0, gp * p1 / tot, jnp.where(lane == 3.0, gp * p2 / tot, 0.0))))

    first_step = (pl.program_id(0) == 0) & (pl.program_id(1) == 0)

    @pl.when(first_step)
    def _():
        counts_ref[...] = jnp.zeros(counts_ref.shape, F32)

    hits = jnp.where(lane == i1, 1.0, 0.0) + jnp.where(lane == i2, 1.0, 0.0)
    counts_ref[...] = counts_ref[...] + jnp.sum(hits, axis=0, keepdims=True)


def xattn_router(x, g_x, w_xq, kv, w_xo, g_ffn, w_r, b_r, batch, seq, n_mem, *, tm=512):
    n, d = x.shape
    per_b = seq // tm
    full = lambda shape: pl.BlockSpec(shape, lambda b, i: (0,) * len(shape))
    row = lambda width: pl.BlockSpec((tm, width), lambda b, i: (b * per_b + i, 0))
    return pl.pallas_call(
        _xattn_router_kernel,
        grid=(batch, per_b),
        in_specs=[row(d), full((1, d)), full((d, X_WIDTH)),
                  pl.BlockSpec((2 * X_HEADS, n_mem, LANES), lambda b, i: (0, b, 0)),
                  full((X_WIDTH, d)), full((1, d)), full((d, LANES)), full((1, LANES))],
        out_specs=[row(d), row(LANES), full((1, LANES))],
        out_shape=[jax.ShapeDtypeStruct((n, d), F32), jax.ShapeDtypeStruct((n, LANES), F32),
                   jax.ShapeDtypeStruct((1, LANES), F32)],
        compiler_params=_cparams(("arbitrary", "arbitrary")),
        name="xattn_router",
    )(x, g_x, w_xq, kv, w_xo, g_ffn, w_r, b_r)


def _plan_kernel(route_ref, off_ref, dest_ref, run_scr):
    @pl.when(pl.program_id(0) == 0)
    def _():
        run_scr[...] = jnp.zeros(run_scr.shape, F32)

    tm = route_ref.shape[0]
    r = route_ref[...]
    lane = lax.broadcasted_iota(jnp.int32, r.shape, 1).astype(F32)
    a1 = jnp.where(lane == r[:, 0:1], 1.0, 0.0)
    a2 = jnp.where(lane == r[:, 1:2], 1.0, 0.0)
    hits = a1 + a2
    earlier = (lax.broadcasted_iota(jnp.int32, (tm, tm), 0) > lax.broadcasted_iota(jnp.int32, (tm, tm), 1))
    before = _dot_halves(jnp.where(earlier, 1.0, 0.0).astype(BF16), hits.astype(BF16))
    slot = off_ref[...] + run_scr[...] + before
    d1 = jnp.sum(a1 * slot, axis=-1, keepdims=True)
    d2 = jnp.sum(a2 * slot, axis=-1, keepdims=True)
    dest_ref[...] = jnp.where(lane == 0.0, d1, jnp.where(lane == 1.0, d2, 0.0)).astype(jnp.int32)
    run_scr[...] = run_scr[...] + jnp.sum(hits, axis=0, keepdims=True)


def dispatch_plan(route, offsets, *, tm=512):
    n = route.shape[0]
    return pl.pallas_call(
        _plan_kernel,
        grid=(n // tm,),
        in_specs=[pl.BlockSpec((tm, LANES), lambda i: (i, 0)), pl.BlockSpec((1, LANES), lambda i: (0, 0))],
        out_specs=pl.BlockSpec((tm, LANES), lambda i: (i, 0)),
        out_shape=jax.ShapeDtypeStruct((n, LANES), jnp.int32),
        scratch_shapes=[pltpu.VMEM((1, LANES), F32)],
        compiler_params=_cparams(("arbitrary",)),
        name="dispatch_plan",
    )(route, offsets)


def _row_copy(src, src_row, dst, dst_row, sem):
    return pltpu.make_async_copy(src.at[pl.ds(src_row, 1), :], dst.at[pl.ds(dst_row, 1), :], sem)


def _dispatch_kernel(zrow_ref, zflag_ref, d1_ref, d2_ref, x_ref, xs_ref, zero_scr, sems):
    tm = x_ref.shape[0]
    tile = zero_scr.shape[0]

    @pl.when(pl.program_id(0) == 0)
    def _():
        zero_scr[...] = jnp.zeros(zero_scr.shape, F32)
        fills = [pltpu.make_async_copy(zero_scr, xs_ref.at[pl.ds(pl.multiple_of(zrow_ref[e], 8), tile), :],
                                       sems.at[2]) for e in range(zrow_ref.shape[0])]
        for e, fill in enumerate(fills):
            pl.when(zflag_ref[e] > 0)(fill.start)
        for e, fill in enumerate(fills):
            pl.when(zflag_ref[e] > 0)(fill.wait)

    def issue(r, carry):
        _row_copy(x_ref, r, xs_ref, d1_ref[r], sems.at[0]).start()
        _row_copy(x_ref, r, xs_ref, d2_ref[r], sems.at[1]).start(priority=1)
        return carry

    lax.fori_loop(0, tm, issue, 0, unroll=8)
    pltpu.make_async_copy(x_ref, xs_ref.at[pl.ds(0, tm), :], sems.at[0]).wait()
    pltpu.make_async_copy(x_ref, xs_ref.at[pl.ds(0, tm), :], sems.at[1]).wait()


def dispatch_rows(x, d1, d2, zrows, zflags, rows_total, tile, *, tm=512):
    n, d = x.shape
    return pl.pallas_call(
        _dispatch_kernel,
        grid_spec=pltpu.PrefetchScalarGridSpec(
            num_scalar_prefetch=2,
            grid=(n // tm,),
            in_specs=[pl.BlockSpec((tm,), lambda i, z, f: (i,), memory_space=pltpu.SMEM),
                      pl.BlockSpec((tm,), lambda i, z, f: (i,), memory_space=pltpu.SMEM),
                      pl.BlockSpec((tm, d), lambda i, z, f: (i, 0))],
            out_specs=pl.BlockSpec(memory_space=pl.ANY),
            scratch_shapes=[pltpu.VMEM((tile, d), F32), pltpu.SemaphoreType.DMA((3,))]),
        out_shape=jax.ShapeDtypeStruct((rows_total, d), F32),
        compiler_params=_cparams(("arbitrary",)),
        name="dispatch_rows",
    )(zrows, zflags, d1, d2, x)


def _experts_kernel(te_ref, na_ref, xs_ref, g_ref, wg_ref, wu_ref, wd_ref, y_ref):
    active = pl.program_id(0) < na_ref[0]

    @pl.when(active)
    def _():
        x = xs_ref[...]
        ms = jnp.mean(x * x, axis=-1, keepdims=True)
        hn = (x * lax.rsqrt(ms + NORM_EPS) * g_ref[...]).astype(BF16)
        gate = _dot(hn, wg_ref[0, 0])
        up = _dot(hn, wu_ref[0, 0])
        hid = (gate * _sigmoid(gate) * up).astype(BF16)
        y_ref[...] = _dot(hid, wd_ref[0, 0])

    @pl.when(jnp.logical_not(active))
    def _():
        y_ref[...] = jnp.zeros(y_ref.shape, F32)


def expert_ffn(xs, g_ffn, w_gate, w_up, w_down, layer, tile_expert, n_active, tile):
    rows_total, d = xs.shape
    dff = w_gate.shape[-1]
    last = lambda i, te, na: jnp.minimum(i, na[0] - 1)
    return pl.pallas_call(
        _experts_kernel,
        grid_spec=pltpu.PrefetchScalarGridSpec(
            num_scalar_prefetch=2,
            grid=(rows_total // tile,),
            in_specs=[pl.BlockSpec((tile, d), lambda i, te, na: (last(i, te, na), 0)),
                      pl.BlockSpec((1, d), lambda i, te, na: (0, 0)),
                      pl.BlockSpec((1, 1, d, dff), lambda i, te, na: (layer, te[i], 0, 0)),
                      pl.BlockSpec((1, 1, d, dff), lambda i, te, na: (layer, te[i], 0, 0)),
                      pl.BlockSpec((1, 1, dff, d), lambda i, te, na: (layer, te[i], 0, 0))],
            out_specs=pl.BlockSpec((tile, d), lambda i, te, na: (i, 0))),
        out_shape=jax.ShapeDtypeStruct((rows_total, d), F32),
        compiler_params=_cparams(("arbitrary",)),
        name="expert_ffn",
    )(tile_expert, n_active, xs, g_ffn, w_gate, w_up, w_down)


def _combine_kernel(*refs, final_norm):
    if final_norm:
        d1_ref, d2_ref, x_ref, route_ref, y_ref, g_ref, o_ref, buf, sems = refs
    else:
        d1_ref, d2_ref, x_ref, route_ref, y_ref, o_ref, buf, sems = refs
    tm = x_ref.shape[0]

    def issue(r, carry):
        _row_copy(y_ref, d1_ref[r], buf.at[0], r, sems.at[0]).start()
        _row_copy(y_ref, d2_ref[r], buf.at[1], r, sems.at[1]).start(priority=1)
        return carry

    lax.fori_loop(0, tm, issue, 0, unroll=8)
    pltpu.make_async_copy(y_ref.at[pl.ds(0, tm), :], buf.at[0], sems.at[0]).wait()
    pltpu.make_async_copy(y_ref.at[pl.ds(0, tm), :], buf.at[1], sems.at[1]).wait()
    route = route_ref[...]
    o = x_ref[...] + (route[:, 2:3] * buf[0] + route[:, 3:4] * buf[1])
    if final_norm:
        ms = jnp.mean(o * o, axis=-1, keepdims=True)
        o = o * lax.rsqrt(ms + NORM_EPS) * g_ref[...]
    o_ref[...] = o


def combine_rows(x, route, y, d1, d2, final_gain=None, *, tm=512):
    n, d = x.shape
    in_specs = [pl.BlockSpec((tm,), lambda i: (i,), memory_space=pltpu.SMEM),
                pl.BlockSpec((tm,), lambda i: (i,), memory_space=pltpu.SMEM),
                pl.BlockSpec((tm, d), lambda i: (i, 0)),
                pl.BlockSpec((tm, LANES), lambda i: (i, 0)),
                pl.BlockSpec(memory_space=pl.ANY)]
    args = [d1, d2, x, route, y]
    if final_gain is not None:
        in_specs.append(pl.BlockSpec((1, d), lambda i: (0, 0)))
        args.append(final_gain)
    return pl.pallas_call(
        functools.partial(_combine_kernel, final_norm=final_gain is not None),
        grid=(n // tm,),
        in_specs=in_specs,
        out_specs=pl.BlockSpec((tm, d), lambda i: (i, 0)),
        out_shape=jax.ShapeDtypeStruct((n, d), F32),
        scratch_shapes=[pltpu.VMEM((2, tm, d), F32), pltpu.SemaphoreType.DMA((2,))],
        compiler_params=_cparams(("arbitrary",)),
        name="combine_rows",
    )(*args)


def routed_experts(x, route, counts, g_ffn, w_gate, w_up, w_down, layer, final_gain=None, *, tile=MOE_TILE):
    n, d = x.shape
    rows_total = 2 * n + N_EXPERTS * tile
    n_tiles = rows_total // tile
    cnt = counts[0, :N_EXPERTS].astype(jnp.int32)
    tiles = (cnt + tile - 1) // tile
    ends = jnp.cumsum(tiles)
    starts = ends - tiles
    n_active = ends[-1:]
    tile_ids = jnp.minimum(jnp.arange(n_tiles, dtype=jnp.int32), n_active[0] - 1)
    tile_expert = jnp.sum((tile_ids[:, None] >= ends[None, :]).astype(jnp.int32), axis=1)
    offsets = jnp.pad((starts * tile).astype(F32)[None, :], ((0, 0), (0, LANES - N_EXPERTS)))
    tail = n_active[0] + jnp.arange(N_EXPERTS, dtype=jnp.int32)
    zrows = (jnp.concatenate([jnp.maximum(ends - 1, starts), jnp.minimum(tail, n_tiles - 1)]) * tile).astype(jnp.int32)
    zflags = jnp.concatenate([tiles > 0, tail < n_tiles]).astype(jnp.int32)

    dest = dispatch_plan(route, offsets)
    d1, d2 = dest[:, 0], dest[:, 1]
    xs = dispatch_rows(x, d1, d2, zrows, zflags, rows_total, tile)
    y = expert_ffn(xs, g_ffn, w_gate, w_up, w_down, layer, tile_expert.astype(jnp.int32),
                   n_active.astype(jnp.int32), tile)
    return combine_rows(x, route, y, d1, d2, final_gain)


def _pad_cols(w, width):
    return jnp.pad(w, ((0, 0), (0, width - w.shape[1])))


def _regroup_kernel(w_ref, o64_ref, o128_ref, opl_ref, omg_ref):
    w = w_ref[0]
    cols = lambda lo, hi: w[:, lo:hi]
    o64_ref[...] = jnp.concatenate(
        [cols(_A_Q, _A_K), cols(_A_K, _A_V), cols(_C_Q, _C_K), cols(_C_K, _C_V)], axis=1).astype(BF16)
    o128_ref[...] = jnp.concatenate(
        [cols(_B_Q, _B_KC), cols(_B_KC, _B_VC), cols(_B_KS, _B_VS), cols(_B_KW, _B_VW)], axis=1).astype(BF16)
    per_group = 3 * B_HEADS // B_KV_GROUPS
    pad = jnp.zeros((w.shape[0], LANES - per_group), w.dtype)
    gate_tiles = []
    for g in range(B_KV_GROUPS):
        gate_tiles += [cols(_B_GATE + g * per_group, _B_GATE + (g + 1) * per_group), pad]
    opl_ref[...] = jnp.concatenate(
        [cols(_A_V, _B_Q), cols(_B_VC, _B_KS), cols(_B_VS, _B_KW), cols(_B_VW, _B_GATE)]
        + gate_tiles + [cols(_C_V, _MERGE)], axis=1).astype(BF16)
    omg_ref[...] = cols(_MERGE, _IN_END).astype(BF16)


def _input_weights(w_all, layer, *, tr=128):
    d = w_all.shape[1]
    widths = (R64_TILES * LANES, R128_TILES * LANES, PL_TILES * LANES, _IN_END - _MERGE)
    return pl.pallas_call(
        _regroup_kernel,
        grid=(d // tr,),
        in_specs=[pl.BlockSpec((1, tr, w_all.shape[2]), lambda i: (layer, i, 0))],
        out_specs=[pl.BlockSpec((tr, wd), lambda i: (i, 0)) for wd in widths],
        out_shape=[jax.ShapeDtypeStruct((d, wd), BF16) for wd in widths],
        compiler_params=_cparams(("parallel",)),
        name="regroup_w_in",
    )(w_all)


def kernel(x, mem, positions, g_mix, w_in, sinks_a, cmp_pos_k, cmp_pos_v, phi_k1, phi_k2, phi_v1, phi_v2,
           lq1, lk1, lq2, lk2, g_diff, w_pa, w_pb, w_pc, w_out, g_x, g_mem, w_xq, w_xkv, w_xo,
           g_ffn, w_group, b_group, w_expert, b_expert, w_gate, w_up, w_down, g_final):
    batch, seq, d = x.shape
    n_mem = mem.shape[1]
    n = batch * seq
    xf = x.reshape(n, d)
    memf = mem.reshape(batch * n_mem, d)
    pos_f = positions.reshape(n, 1).astype(F32)
    tab64 = rope_tables(pos_f, A_HEAD_DIM)
    tab128 = rope_tables(pos_f, B_HEAD_DIM)
    row = lambda v: v.reshape(1, -1)
    wg_all, wu_all, wd_all = w_gate.astype(BF16), w_up.astype(BF16), w_down.astype(BF16)

    for l in range(DEPTH):
        lambda_init = 0.8 - 0.6 * math.exp(-0.3 * l)
        w64, w128, wpl, wmg = _input_weights(w_in, l)
        g = row(g_mix[l])
        p64 = norm_proj(xf, g, w64, tm=512, tn=w64.shape[1], rope=64, tables=tab64)
        p128 = norm_proj(xf, g, w128, tm=512, tn=w128.shape[1], rope=128, tables=tab128)
        ppl = norm_proj(xf, g, wpl, tm=512, tn=wpl.shape[1])
        gates = norm_proj(xf, g, wmg, tm=1024, tn=1024, sigmoid_out=True)

        o_a = swa_sink_attention(p64, ppl, sinks_a[l], batch, seq)
        kcmp = compress_blocks(p128[R128_KC:R128_KC + B_KV_GROUPS], cmp_pos_k[l], phi_k1[l], phi_k2[l], batch, seq)
        vcmp = compress_blocks(ppl[PL_VC:PL_VC + B_KV_GROUPS], cmp_pos_v[l], phi_v1[l], phi_v2[l], batch, seq)
        o_b = nsa_attention(p128, ppl, kcmp, vcmp, batch, seq)
        lam_params = jnp.stack([lq1[l], lk1[l], lq2[l], lk2[l]])
        o_c = diff_attention(p64, ppl, lam_params, row(g_diff[l]), lambda_init, batch, seq)

        merged = merge_branches(o_a, o_b, o_c, w_pa[l].astype(BF16), w_pb[l].astype(BF16),
                                w_pc[l].astype(BF16), gates)
        xf = matmul_residual(merged, w_out[l].astype(BF16), xf)

        kv = norm_proj(memf, row(g_mem[l]), w_xkv[l].astype(BF16), tm=512, tn=2 * X_WIDTH)
        w_r = _pad_cols(jnp.concatenate([w_expert[l], w_group[l]], axis=1), LANES).astype(BF16)
        b_r = _pad_cols(jnp.concatenate([b_expert[l], b_group[l]])[None, :], LANES)
        xf, route, counts = xattn_router(xf, row(g_x[l]), w_xq[l].astype(BF16), kv, w_xo[l].astype(BF16),
                                         row(g_ffn[l]), w_r, b_r, batch, seq, n_mem)
        xf = routed_experts(xf, route, counts, row(g_ffn[l]), wg_all, wu_all, wd_all, l,
                            final_gain=row(g_final) if l == DEPTH - 1 else None)

    return xf.reshape(batch, seq, d)
```
